```python
import math
import jax, jax.numpy as jnp
from jax import lax
import numpy as np

D_MODEL = 1024
BATCH = 32
SEQ = 2048
DEPTH = 2

N_BRANCH = 4
BRANCH_WIDTH = D_MODEL // 4
HEAD_DIM = 64
N_HEADS = BRANCH_WIDTH // HEAD_DIM
CONV_WIDTH = 4
GDN_CHUNK = 64
MOBA_BLOCK = 256
MOBA_TOPK = 3
MOBA_QCHUNK = 32
SB_QBLOCK = 128
SSD_STATE = 128
SSD_GROUPS = 2
SSD_CHUNK = 128
D_FF = 4 * D_MODEL
EPS = 1e-6

GDN_IN = 4 * BRANCH_WIDTH + 2 * N_HEADS
MOBA_IN = 3 * BRANCH_WIDTH
SB_IN = 3 * BRANCH_WIDTH
SSD_CONV_DIM = BRANCH_WIDTH + 2 * SSD_GROUPS * SSD_STATE
SSD_IN = BRANCH_WIDTH + SSD_CONV_DIM + N_HEADS
OFF_GDN = 0
OFF_MOBA = OFF_GDN + GDN_IN
OFF_SB = OFF_MOBA + MOBA_IN
OFF_SSD = OFF_SB + SB_IN
IN_WIDTH = OFF_SSD + SSD_IN

kernel_name = "hybrid_gated_parallel_mixers"


def rms_norm(x, gain):
    xf = x.astype(jnp.float32)
    y = xf * lax.rsqrt(jnp.mean(xf * xf, axis=-1, keepdims=True) + EPS)
    return (y * gain.astype(jnp.float32)).astype(x.dtype)


def l2_normalize(x):
    return x * lax.rsqrt(jnp.sum(x * x, axis=-1, keepdims=True) + EPS)


def causal_dwconv(x, w):
    return lax.conv_general_dilated(
        x, w.astype(jnp.float32)[:, None, :], window_strides=(1,),
        padding=[(CONV_WIDTH - 1, 0)], dimension_numbers=('NWC', 'WIO', 'NWC'),
        feature_group_count=x.shape[-1])


def chunk_gated_delta_rule(q, k, v, g, beta):
    Bsz, T, H, D = q.shape
    C = GDN_CHUNK
    NC = T // C
    q = l2_normalize(q) * (D ** -0.5)
    k = l2_normalize(k)

    def to_chunks(a):
        return a.reshape(Bsz, NC, C, H, D).transpose(0, 3, 1, 2, 4)

    q, k, v = to_chunks(q), to_chunks(k), to_chunks(v)
    g = g.reshape(Bsz, NC, C, H).transpose(0, 3, 1, 2)
    beta = beta.reshape(Bsz, NC, C, H).transpose(0, 3, 1, 2)
    gc = jnp.cumsum(g, axis=-1)
    idx = jnp.arange(C)
    incl = idx[:, None] >= idx[None, :]
    strict = idx[:, None] > idx[None, :]
    decay = jnp.exp(jnp.where(incl, gc[..., :, None] - gc[..., None, :], -jnp.inf))
    kb = k * beta[..., None]
    vb = v * beta[..., None]
    a_kk = jnp.where(strict, jnp.einsum('bhncd,bhnsd->bhncs', kb, k) * decay, 0.0)
    tmat = a_kk + jnp.eye(C, dtype=jnp.float32)
    u = lax.linalg.triangular_solve(tmat, vb, left_side=True, lower=True)
    w = lax.linalg.triangular_solve(tmat, kb * jnp.exp(gc)[..., None], left_side=True, lower=True)
    a_qk = jnp.einsum('bhncd,bhnsd->bhncs', q, k) * decay
    q_dec = q * jnp.exp(gc)[..., None]
    k_dec = k * jnp.exp(gc[..., -1:] - gc)[..., None]
    g_tot = jnp.exp(gc[..., -1])

    def step(S, xs):
        u_c, w_c, q_c, k_c, a_c, gt = xs
        v_new = u_c - jnp.einsum('bhcd,bhde->bhce', w_c, S)
        o = jnp.einsum('bhcd,bhde->bhce', q_c, S) + jnp.einsum('bhcs,bhse->bhce', a_c, v_new)
        S = S * gt[..., None, None] + jnp.einsum('bhcd,bhce->bhde', k_c, v_new)
        return S, o

    xs = tuple(jnp.moveaxis(a, 2, 0) for a in (u, w, q_dec, k_dec, a_qk, g_tot))
    S0 = jnp.zeros((Bsz, H, D, D), jnp.float32)
    _, o = lax.scan(step, S0, xs)
    return o.transpose(1, 0, 3, 2, 4).reshape(Bsz, T, H, D)


def gdn_branch(p, conv_w, a_log, dt_bias, norm_w):
    Bsz, T, _ = p.shape
    W = BRANCH_WIDTH
    qkv = jax.nn.silu(causal_dwconv(p[..., :3 * W], conv_w))
    q = qkv[..., :W].reshape(Bsz, T, N_HEADS, HEAD_DIM)
    k = qkv[..., W:2 * W].reshape(Bsz, T, N_HEADS, HEAD_DIM)
    v = qkv[..., 2 * W:].reshape(Bsz, T, N_HEADS, HEAD_DIM)
    z = p[..., 3 * W:4 * W].reshape(Bsz, T, N_HEADS, HEAD_DIM)
    beta = jax.nn.sigmoid(p[..., 4 * W:4 * W + N_HEADS])
    g = -jnp.exp(a_log.astype(jnp.float32)) * jax.nn.softplus(
        p[..., 4 * W + N_HEADS:] + dt_bias.astype(jnp.float32))
    o = chunk_gated_delta_rule(q, k, v, g, beta)
    o = rms_norm(o, norm_w) * jax.nn.silu(z)
    return o.reshape(Bsz, T, W)


def alibi_slopes(n_heads):
    return jnp.asarray([2.0 ** (-8.0 * (i + 1) / n_heads) for i in range(n_heads)], jnp.float32)


def moba_attention(q, k, v):
    Bsz, T, H, D = q.shape
    BS = MOBA_BLOCK
    Tp = -(-T // BS) * BS
    pad = ((0, 0), (0, Tp - T), (0, 0), (0, 0))
    q, k, v = (jnp.pad(a, pad).transpose(0, 2, 1, 3) for a in (q, k, v))
    NB = Tp // BS
    scale = D ** -0.5
    slopes = alibi_slopes(H)
    kb = k.reshape(Bsz, H, NB, BS, D)
    vb = v.reshape(Bsz, H, NB, BS, D)
    kmean = jnp.mean(kb, axis=3)
    gate = jnp.einsum('bhtd,bhnd->bhtn', q, kmean)
    qblk = jnp.arange(Tp) // BS
    past = jnp.arange(NB)[None, :] < qblk[:, None]
    gate = jnp.where(past, gate, -jnp.inf)
    n_sel = min(MOBA_TOPK, NB)
    _, sel = lax.top_k(gate, n_sel)
    sel_valid = sel < qblk[:, None]

    QC = MOBA_QCHUNK
    nq = Tp // QC

    def chunked(a):
        return jnp.moveaxis(a.reshape(Bsz, H, nq, QC, *a.shape[3:]), 2, 0)

    bi = jnp.arange(Bsz)[:, None, None, None]
    hi = jnp.arange(H)[None, :, None, None]

    def step(args):
        qc, sc, vc, c = args
        t = c * QC + jnp.arange(QC)
        own = (c * QC) // BS
        ks = kb[bi, hi, sc]
        vs = vb[bi, hi, sc]
        s_pos = sc[..., None] * BS + jnp.arange(BS)
        dist = (t[None, None, :, None, None] - s_pos).astype(jnp.float32)
        s_sel = jnp.einsum('bhqd,bhqksd->bhqks', qc, ks) * scale - slopes[None, :, None, None, None] * dist
        s_sel = jnp.where(vc[..., None], s_sel, -jnp.inf)
        k_own = lax.dynamic_index_in_dim(kb, own, axis=2, keepdims=False)
        v_own = lax.dynamic_index_in_dim(vb, own, axis=2, keepdims=False)
        own_pos = own * BS + jnp.arange(BS)
        dist_own = (t[:, None] - own_pos[None, :]).astype(jnp.float32)
        s_own = jnp.einsum('bhqd,bhsd->bhqs', qc, k_own) * scale - slopes[None, :, None, None] * dist_own
        s_own = jnp.where(own_pos[None, :] <= t[:, None], s_own, -jnp.inf)
        scores = jnp.concatenate([s_sel.reshape(Bsz, H, QC, n_sel * BS), s_own], axis=-1)
        prob = jax.nn.softmax(scores, axis=-1)
        p_sel = prob[..., :n_sel * BS].reshape(Bsz, H, QC, n_sel, BS)
        p_own = prob[..., n_sel * BS:]
        return jnp.einsum('bhqks,bhqksd->bhqd', p_sel, vs) + jnp.einsum('bhqs,bhsd->bhqd', p_own, v_own)

    out = lax.map(step, (chunked(q), chunked(sel), chunked(sel_valid), jnp.arange(nq)))
    out = jnp.moveaxis(out, 0, 2).reshape(Bsz, H, Tp, D)[:, :, :T]
    return out.transpose(0, 2, 1, 3)


def moba_branch(p):
    Bsz, T, _ = p.shape
    W = BRANCH_WIDTH
    q, k, v = (p[..., i * W:(i + 1) * W].reshape(Bsz, T, N_HEADS, HEAD_DIM) for i in range(3))
    return moba_attention(q, k, v).reshape(Bsz, T, W)


def stick_breaking_attention(q, k, v):
    q, k, v = (a.transpose(0, 2, 1, 3) for a in (q, k, v))
    T = q.shape[2]
    scale = q.shape[-1] ** -0.5
    outs = []
    for i in range(T // SB_QBLOCK):
        t0, t1 = i * SB_QBLOCK, (i + 1) * SB_QBLOCK
        z = jnp.einsum('bhqd,bhsd->bhqs', q[:, :, t0:t1], k[:, :, :t1]) * scale
        causal = jnp.arange(t1)[None, :] < jnp.arange(t0, t1)[:, None]
        log_1m = jnp.where(causal, jax.nn.log_sigmoid(-z), 0.0)
        after = lax.cumsum(log_1m, axis=3, reverse=True) - log_1m
        weight = jnp.exp(jnp.where(causal, jax.nn.log_sigmoid(z) + after, -jnp.inf))
        outs.append(jnp.einsum('bhqs,bhsd->bhqd', weight, v[:, :, :t1]))
    return jnp.concatenate(outs, axis=2).transpose(0, 2, 1, 3)


def sb_branch(p):
    Bsz, T, _ = p.shape
    W = BRANCH_WIDTH
    q, k, v = (p[..., i * W:(i + 1) * W].reshape(Bsz, T, N_HEADS, HEAD_DIM) for i in range(3))
    return stick_breaking_attention(q, k, v).reshape(Bsz, T, W)


def ssd_chunked(xh, dt, a, bm, cm):
    Bsz, T, H, P = xh.shape
    N = bm.shape[-1]
    L = SSD_CHUNK
    NC = T // L
    xc = xh.reshape(Bsz, NC, L, H, P)
    dtc = dt.reshape(Bsz, NC, L, H)
    bc = bm.reshape(Bsz, NC, L, H, N)
    cc = cm.reshape(Bsz, NC, L, H, N)
    acs = jnp.cumsum(jnp.moveaxis(dtc * a, 3, 1), axis=-1)
    idx = jnp.arange(L)
    incl = idx[:, None] >= idx[None, :]
    lmat = jnp.exp(jnp.where(incl, acs[..., :, None] - acs[..., None, :], -jnp.inf))
    xdt = xc * dtc[..., None]
    scores = jnp.einsum('bclhn,bcshn->bhcls', cc, bc) * lmat
    y_diag = jnp.einsum('bhcls,bcshp->bclhp', scores, xdt)
    decay_states = jnp.exp(acs[..., -1:] - acs)
    states = jnp.einsum('bclhn,bhcl,bclhp->bchpn', bc, decay_states, xdt)
    chunk_decay = jnp.exp(acs[..., -1])

    def step(h, inp):
        st, dec = inp
        return h * dec[..., None, None] + st, h

    h0 = jnp.zeros((Bsz, H, P, N), jnp.float32)
    _, h_prev = lax.scan(step, h0, (jnp.moveaxis(states, 1, 0), jnp.moveaxis(chunk_decay, 2, 0)))
    h_prev = jnp.moveaxis(h_prev, 0, 1)
    y_off = jnp.einsum('bclhn,bchpn,bhcl->bclhp', cc, h_prev, jnp.exp(acs))
    return (y_diag + y_off).reshape(Bsz, T, H, P)


def ssd_branch(p, conv_w, conv_b, a_log, dt_bias, d_skip, norm_w):
    Bsz, T, _ = p.shape
    W = BRANCH_WIDTH
    GN = SSD_GROUPS * SSD_STATE
    z = p[..., :W]
    xbc = jax.nn.silu(causal_dwconv(p[..., W:W + SSD_CONV_DIM], conv_w) + conv_b.astype(jnp.float32))
    dt = jax.nn.softplus(p[..., W + SSD_CONV_DIM:] + dt_bias.astype(jnp.float32))
    xh = xbc[..., :W].reshape(Bsz, T, N_HEADS, HEAD_DIM)
    rep = N_HEADS // SSD_GROUPS
    bm = jnp.repeat(xbc[..., W:W + GN].reshape(Bsz, T, SSD_GROUPS, SSD_STATE), rep, axis=2)
    cm = jnp.repeat(xbc[..., W + GN:].reshape(Bsz, T, SSD_GROUPS, SSD_STATE), rep, axis=2)
    a = -jnp.exp(a_log.astype(jnp.float32))
    y = ssd_chunked(xh, dt, a, bm, cm) + d_skip.astype(jnp.float32)[:, None] * xh
    return rms_norm(y.reshape(Bsz, T, W) * jax.nn.silu(z), norm_w)


def setup_inputs(seed: int = 0) -> dict:
    key = jax.random.key(seed)
    ks = jax.random.split(key, 24)
    f32 = jnp.float32
    nrm = lambda k, shape, s: jax.random.normal(k, shape, f32) * s
    gain = lambda k, shape: 1.0 + 0.05 * jax.random.normal(k, shape, f32)

    def dt_bias_init(k):
        dt0 = jnp.exp(jax.random.uniform(k, (DEPTH, N_HEADS), f32, math.log(1e-3), math.log(1e-1)))
        return dt0 + jnp.log(-jnp.expm1(-dt0))

    def a_log_init(k):
        return jnp.log(jax.random.uniform(k, (DEPTH, N_HEADS), f32, 1.0, 16.0))

    return {
        'x': jax.random.normal(ks[0], (BATCH, SEQ, D_MODEL), f32),
        'norm_mix_pre': gain(ks[1], (DEPTH, D_MODEL)),
        'norm_mix_post': gain(ks[2], (DEPTH, D_MODEL)),
        'norm_ffn_pre': gain(ks[3], (DEPTH, D_MODEL)),
        'norm_ffn_post': gain(ks[4], (DEPTH, D_MODEL)),
        'w_in': nrm(ks[5], (DEPTH, D_MODEL, IN_WIDTH), D_MODEL ** -0.5),
        'gdn_conv': nrm(ks[6], (DEPTH, CONV_WIDTH, 3 * BRANCH_WIDTH), CONV_WIDTH ** -0.5),
        'gdn_a_log': a_log_init(ks[7]),
        'gdn_dt_bias': dt_bias_init(ks[8]),
        'gdn_norm': gain(ks[9], (DEPTH, HEAD_DIM)),
        'ssd_conv': nrm(ks[10], (DEPTH, CONV_WIDTH, SSD_CONV_DIM), CONV_WIDTH ** -0.5),
        'ssd_conv_bias': nrm(ks[11], (DEPTH, SSD_CONV_DIM), 0.02),
        'ssd_a_log': a_log_init(ks[12]),
        'ssd_dt_bias': dt_bias_init(ks[13]),
        'ssd_d': gain(ks[14], (DEPTH, N_HEADS)),
        'ssd_norm': gain(ks[15], (DEPTH, BRANCH_WIDTH)),
        'w_gate': nrm(ks[16], (DEPTH, N_BRANCH, D_MODEL, D_MODEL), D_MODEL ** -0.5),
        'w_branch': nrm(ks[17], (DEPTH, N_BRANCH, BRANCH_WIDTH, D_MODEL), BRANCH_WIDTH ** -0.5),
        'w_out': nrm(ks[18], (DEPTH, D_MODEL, D_MODEL), D_MODEL ** -0.5),
        'w_up': nrm(ks[19], (DEPTH, D_MODEL, D_FF), D_MODEL ** -0.5),
        'w_down': nrm(ks[20], (DEPTH, D_FF, D_MODEL), D_FF ** -0.5),
    }


def reference(x, norm_mix_pre, norm_mix_post, norm_ffn_pre, norm_ffn_post, w_in,
              gdn_conv, gdn_a_log, gdn_dt_bias, gdn_norm,
              ssd_conv, ssd_conv_bias, ssd_a_log, ssd_dt_bias, ssd_d, ssd_norm,
              w_gate, w_branch, w_out, w_up, w_down):
    for l in range(DEPTH):
        h = rms_norm(x, norm_mix_pre[l])
        p = jnp.matmul(h, w_in[l]).astype(jnp.float32)
        y_a = gdn_branch(p[..., OFF_GDN:OFF_MOBA], gdn_conv[l], gdn_a_log[l], gdn_dt_bias[l], gdn_norm[l])
        y_b = moba_branch(p[..., OFF_MOBA:OFF_SB])
        y_c = sb_branch(p[..., OFF_SB:OFF_SSD])
        y_d = ssd_branch(p[..., OFF_SSD:IN_WIDTH], ssd_conv[l], ssd_conv_bias[l], ssd_a_log[l],
                         ssd_dt_bias[l], ssd_d[l], ssd_norm[l])
        branches = (y_a, y_b, y_c, y_d)
        merged = jax.nn.sigmoid(jnp.matmul(h, w_gate[l, 0])) * jnp.matmul(branches[0].astype(x.dtype), w_branch[l, 0])
        for g in range(1, N_BRANCH):
            gate = jax.nn.sigmoid(jnp.matmul(h, w_gate[l, g]))
            merged = merged + gate * jnp.matmul(branches[g].astype(x.dtype), w_branch[l, g])
        mix = jnp.matmul(merged, w_out[l])
        x = x + rms_norm(mix, norm_mix_post[l])
        h = rms_norm(x, norm_ffn_pre[l])
        f = jnp.matmul(jnp.square(jax.nn.relu(jnp.matmul(h, w_up[l]))), w_down[l])
        x = x + rms_norm(f, norm_ffn_post[l])
    return x
```

```python
import functools

import jax
import jax.numpy as jnp
from jax import lax
from jax.experimental import pallas as pl
from jax.experimental.pallas import tpu as pltpu

F32 = jnp.float32
BF16 = jnp.bfloat16
HIGHEST = lax.Precision.HIGHEST

D_MODEL = 1024
N_BRANCH = 4
BRANCH_WIDTH = 256
HEAD_DIM = 64
N_HEADS = 4
CONV_WIDTH = 4
GDN_CHUNK = 64
MOBA_BLOCK = 256
MOBA_TOPK = 3
SSD_STATE = 128
SSD_GROUPS = 2
SSD_CHUNK = 128
D_FF = 4 * D_MODEL
EPS = 1e-6

LANES = 128
PAIR = LANES // HEAD_DIM
SMALL_PAD = LANES
CONV_HALO = 8
TOKEN_TILE = 512
FF_CHUNK = 1024
SB_TILE = 128
VMEM_LIMIT = 56 * 1024 * 1024

GDN_W = 3 * BRANCH_WIDTH + BRANCH_WIDTH + SMALL_PAD
ATT_W = 6 * BRANCH_WIDTH
SSD_CONV_DIM = BRANCH_WIDTH + 2 * SSD_GROUPS * SSD_STATE
SSD_W = SSD_CONV_DIM + BRANCH_WIDTH + SMALL_PAD


def _bdot(a, b):
    return jnp.dot(a.astype(BF16), b.astype(BF16), preferred_element_type=F32)


def _bdot_nt(a, b):
    return lax.dot_general(a.astype(BF16), b.astype(BF16), (((1,), (1,)), ((), ())),
                           preferred_element_type=F32)


def _hdot(a, b):
    return jnp.dot(a, b, preferred_element_type=F32, precision=HIGHEST)


def _sigmoid(x):
    return 1.0 / (1.0 + jnp.exp(-x))


def _silu(x):
    return x * _sigmoid(x)


def _softplus(x):
    return jnp.maximum(x, 0.0) + jnp.log(1.0 + jnp.exp(-jnp.abs(x)))


def _rms(x, gain):
    return x * lax.rsqrt(jnp.mean(x * x, axis=-1, keepdims=True) + EPS) * gain


def _transpose_square(x):
    r, c = x.shape
    if c < LANES:
        x = jnp.concatenate([x, jnp.zeros((r, LANES - c), F32)], axis=1)
    if r < LANES:
        x = jnp.concatenate([x, jnp.zeros((LANES - r, LANES), F32)], axis=0)
    return x.T[:c, :r]


def _params(sem):
    return pltpu.CompilerParams(dimension_semantics=sem, vmem_limit_bytes=VMEM_LIMIT)


def _const_spec(shape):
    nd = len(shape)
    return pl.BlockSpec(shape, lambda *_: (0,) * nd, pipeline_mode=pl.Buffered(1))


def _in_proj_kernel(x_ref, gain_ref, wg_ref, wa_ref, ws_ref, og_ref, oa_ref, os_ref):
    h = _rms(x_ref[...], gain_ref[...]).astype(BF16)
    og_ref[...] = jnp.dot(h, wg_ref[...], preferred_element_type=F32)
    oa_ref[...] = jnp.dot(h, wa_ref[...], preferred_element_type=F32).astype(BF16)
    os_ref[...] = jnp.dot(h, ws_ref[...], preferred_element_type=F32)


def _in_proj(x, gain, wg, wa, ws):
    n = x.shape[0]
    tm = TOKEN_TILE
    row = lambda w: pl.BlockSpec((tm, w), lambda i: (i, 0))
    return pl.pallas_call(
        _in_proj_kernel,
        grid=(n // tm,),
        in_specs=[row(D_MODEL), _const_spec((1, D_MODEL)), _const_spec(wg.shape),
                  _const_spec(wa.shape), _const_spec(ws.shape)],
        out_specs=[row(GDN_W), row(ATT_W), row(SSD_W)],
        out_shape=[jax.ShapeDtypeStruct((n, GDN_W), F32), jax.ShapeDtypeStruct((n, ATT_W), BF16),
                   jax.ShapeDtypeStruct((n, SSD_W), F32)],
        compiler_params=_params(("parallel",)),
        name="in_proj",
    )(x, gain, wg, wa, ws)


def _conv_silu_into(src_ref, w_ref, bias, dst_ref, seq, chunk):
    w = w_ref[...]
    taps = [w[k:k + 1, :] for k in range(CONV_WIDTH)]

    def finish(acc):
        if bias is not None:
            acc = acc + bias
        return _silu(acc)

    blk = src_ref[0:chunk, :]
    rows = lax.broadcasted_iota(jnp.int32, (chunk, 1), 0)
    acc = taps[CONV_WIDTH - 1] * blk
    for k in range(CONV_WIDTH - 1):
        shift = CONV_WIDTH - 1 - k
        acc = acc + taps[k] * jnp.where(rows >= shift, pltpu.roll(blk, shift, 0), 0.0)
    dst_ref[0:chunk, :] = finish(acc)

    def body(c, carry):
        r0 = pl.multiple_of(c * chunk, chunk)
        blk = src_ref[pl.ds(r0 - CONV_HALO, chunk + CONV_HALO), :]
        acc = taps[CONV_WIDTH - 1] * blk[CONV_HALO:]
        for k in range(CONV_WIDTH - 1):
            acc = acc + taps[k] * pltpu.roll(blk, CONV_WIDTH - 1 - k, 0)[CONV_HALO:]
        dst_ref[pl.ds(r0, chunk), :] = finish(acc)
        return carry

    lax.fori_loop(1, seq // chunk, body, 0)


def _unit_lower_inverse(a):
    c = a.shape[0]
    ri = lax.broadcasted_iota(jnp.int32, (c, c), 0)
    ci = lax.broadcasted_iota(jnp.int32, (c, c), 1)
    p = jnp.where(ri == ci, 1.0, 0.0) - a
    b = _hdot(a, a)
    power = 2
    while 2 * power < c:
        pb = _hdot(jnp.concatenate([p, b], axis=0), b)
        p = p + pb[:c]
        b = pb[c:]
        power *= 2
    return p + _hdot(p, b)


def _gdn_kernel(qkv_ref, z_ref, bg_ref, convw_ref, alog_ref, dtb_ref, norm_ref, o_ref, conv_ref, s_ref, *, seq):
    C = GDN_CHUNK
    _conv_silu_into(qkv_ref, convw_ref, None, conv_ref, seq, 4 * C)
    s_ref[...] = jnp.zeros_like(s_ref)

    ri = lax.broadcasted_iota(jnp.int32, (C, C), 0)
    ci = lax.broadcasted_iota(jnp.int32, (C, C), 1)
    incl = ri >= ci
    strict = ri > ci
    tril = jnp.where(incl, 1.0, 0.0)
    neg_a = -jnp.exp(alog_ref[...])
    dtb = dtb_ref[...]
    gain = norm_ref[...]

    def body(c, carry):
        r0 = pl.multiple_of(c * C, C)
        qkv = conv_ref[pl.ds(r0, C), :]
        zt = z_ref[pl.ds(r0, C), :]
        bg = bg_ref[pl.ds(r0, C), :]
        beta_t = _sigmoid(bg)
        g_t = neg_a * _softplus(bg + dtb)
        gc = _hdot(tril, g_t)
        gc_rows = _transpose_square(gc)
        g_last = gc[C - 1:C, :]
        e_gc = jnp.exp(gc)
        e_rem = jnp.exp(g_last - gc)
        e_last = jnp.exp(g_last)
        outs = []
        for h in range(N_HEADS):
            lo, hi = h * HEAD_DIM, (h + 1) * HEAD_DIM
            q = qkv[:, lo:hi]
            k = qkv[:, BRANCH_WIDTH + lo:BRANCH_WIDTH + hi]
            v = qkv[:, 2 * BRANCH_WIDTH + lo:2 * BRANCH_WIDTH + hi]
            qn = q * lax.rsqrt(jnp.sum(q * q, axis=-1, keepdims=True) + EPS) * (HEAD_DIM ** -0.5)
            kn = k * lax.rsqrt(jnp.sum(k * k, axis=-1, keepdims=True) + EPS)
            beta = beta_t[:, h:h + 1]
            gl = N_HEADS + h
            dm = jnp.exp(jnp.where(incl, gc[:, gl:gl + 1] - gc_rows[gl:gl + 1, :], -jnp.inf))
            kb = kn * beta
            vb = v * beta
            qk = _bdot_nt(jnp.concatenate([qn, kb], axis=0), kn)
            a_qk = qk[:C] * dm
            a_kk = jnp.where(strict, qk[C:] * dm, 0.0)
            t_inv = _unit_lower_inverse(a_kk)
            uw = _hdot(t_inv, jnp.concatenate([vb, kb * e_gc[:, gl:gl + 1]], axis=1))
            u = uw[:, :HEAD_DIM]
            w = uw[:, HEAD_DIM:]
            s = s_ref[h]
            ws_qs = _bdot(jnp.concatenate([w, qn * e_gc[:, gl:gl + 1]], axis=0), s)
            v_new = u - ws_qs[:C]
            o = ws_qs[C:] + _bdot(a_qk, v_new)
            k_dec_t = _transpose_square(kn * e_rem[:, gl:gl + 1])
            s_ref[h] = s * e_last[:, gl:gl + 1] + _bdot(k_dec_t, v_new)
            outs.append(_rms(o, gain) * _silu(zt[:, lo:hi]))
        o_ref[pl.ds(r0, C), :] = jnp.concatenate(outs, axis=1).astype(o_ref.dtype)
        return carry

    lax.fori_loop(0, seq // C, body, 0)


def _gdn(pg, conv_w, a_row, dtb_row, norm_row, batch, seq):
    kern = functools.partial(_gdn_kernel, seq=seq)
    return pl.pallas_call(
        kern,
        grid=(batch,),
        in_specs=[pl.BlockSpec((seq, 3 * BRANCH_WIDTH), lambda b: (b, 0)),
                  pl.BlockSpec((seq, BRANCH_WIDTH), lambda b: (b, 3)),
                  pl.BlockSpec((seq, SMALL_PAD), lambda b: (b, 4 * BRANCH_WIDTH // SMALL_PAD)),
                  _const_spec(conv_w.shape), _const_spec(a_row.shape), _const_spec(dtb_row.shape),
                  _const_spec(norm_row.shape)],
        out_specs=pl.BlockSpec((seq, BRANCH_WIDTH), lambda b: (b, 0)),
        out_shape=jax.ShapeDtypeStruct((batch * seq, BRANCH_WIDTH), BF16),
        scratch_shapes=[pltpu.VMEM((seq, 3 * BRANCH_WIDTH), F32),
                        pltpu.VMEM((N_HEADS, HEAD_DIM, HEAD_DIM), F32)],
        compiler_params=_params(("parallel",)),
        name="gdn",
    )(pg, pg, pg, conv_w, a_row, dtb_row, norm_row)


def _ssd_kernel(xbc_ref, z_ref, dt_ref, convw_ref, convb_ref, alog_ref, dtb_ref, dskip_ref, norm_ref,
                o_ref, conv_ref, s_ref, *, seq):
    L = SSD_CHUNK
    W = BRANCH_WIDTH
    GN = SSD_GROUPS * SSD_STATE
    _conv_silu_into(xbc_ref, convw_ref, convb_ref[...], conv_ref, seq, L)
    s_ref[...] = jnp.zeros_like(s_ref)

    ri = lax.broadcasted_iota(jnp.int32, (L, L), 0)
    ci = lax.broadcasted_iota(jnp.int32, (L, L), 1)
    incl = ri >= ci
    tril = jnp.where(incl, 1.0, 0.0)
    first = lax.broadcasted_iota(jnp.int32, (1, LANES), 1) < HEAD_DIM
    neg_a = -jnp.exp(alog_ref[...])
    dtb = dtb_ref[...]
    dskip = dskip_ref[...]
    gain = norm_ref[...]

    def body(c, carry):
        r0 = pl.multiple_of(c * L, L)
        xbc = conv_ref[pl.ds(r0, L), :]
        zt = z_ref[pl.ds(r0, L), :]
        dt = _softplus(dt_ref[pl.ds(r0, L), :] + dtb)
        acs = _hdot(tril, dt * neg_a)
        acs_rows = acs.T
        a_last = acs[L - 1:L, :]
        e_acs = jnp.exp(acs)
        dt_rem = dt * jnp.exp(a_last - acs)
        e_last = jnp.exp(a_last)
        ys = []
        for g in range(SSD_GROUPS):
            h0, h1 = PAIR * g, PAIR * g + 1
            x2 = xbc[:, g * LANES:(g + 1) * LANES]
            bm = xbc[:, W + g * SSD_STATE:W + (g + 1) * SSD_STATE]
            cm = xbc[:, W + GN + g * SSD_STATE:W + GN + (g + 1) * SSD_STATE]
            cb = _bdot_nt(cm, bm)
            xdt = x2 * jnp.where(first, dt[:, h0:h0 + 1], dt[:, h1:h1 + 1])
            yd = []
            for h in (h0, h1):
                lmat = jnp.exp(jnp.where(incl, acs[:, h:h + 1] - acs_rows[h:h + 1, :], -jnp.inf))
                yd.append(_bdot(cb * lmat, xdt))
            y_diag = jnp.where(first, yd[0], yd[1])
            s = s_ref[g]
            y_off = _bdot(cm, s) * jnp.where(first, e_acs[:, h0:h0 + 1], e_acs[:, h1:h1 + 1])
            x_rem = x2 * jnp.where(first, dt_rem[:, h0:h0 + 1], dt_rem[:, h1:h1 + 1])
            s_ref[g] = (s * jnp.where(first, e_last[:, h0:h0 + 1], e_last[:, h1:h1 + 1])
                        + _bdot(bm.T, x_rem))
            ys.append(y_diag + y_off)
        y = jnp.concatenate(ys, axis=1) + dskip * xbc[:, :W]
        o_ref[pl.ds(r0, L), :] = _rms(y * _silu(zt), gain).astype(o_ref.dtype)
        return carry

    lax.fori_loop(0, seq // L, body, 0)


def _ssd(ps, conv_w, conv_b, a_row, dtb_row, dskip_row, norm_row, batch, seq):
    kern = functools.partial(_ssd_kernel, seq=seq)
    return pl.pallas_call(
        kern,
        grid=(batch,),
        in_specs=[pl.BlockSpec((seq, SSD_CONV_DIM), lambda b: (b, 0)),
                  pl.BlockSpec((seq, BRANCH_WIDTH), lambda b: (b, SSD_CONV_DIM // BRANCH_WIDTH)),
                  pl.BlockSpec((seq, SMALL_PAD), lambda b: (b, (SSD_CONV_DIM + BRANCH_WIDTH) // SMALL_PAD)),
                  _const_spec(conv_w.shape), _const_spec(conv_b.shape), _const_spec(a_row.shape),
                  _const_spec(dtb_row.shape), _const_spec(dskip_row.shape), _const_spec(norm_row.shape)],
        out_specs=pl.BlockSpec((seq, BRANCH_WIDTH), lambda b: (b, 0)),
        out_shape=jax.ShapeDtypeStruct((batch * seq, BRANCH_WIDTH), BF16),
        scratch_shapes=[pltpu.VMEM((seq, SSD_CONV_DIM), F32),
                        pltpu.VMEM((SSD_GROUPS, SSD_STATE, LANES), F32)],
        compiler_params=_params(("parallel",)),
        name="ssd",
    )(ps, ps, ps, conv_w, conv_b, a_row, dtb_row, dskip_row, norm_row)


def _head_masked(q, first_half):
    lane = lax.broadcasted_iota(jnp.int32, (1, LANES), 1)
    keep = (lane < HEAD_DIM) if first_half else (lane >= HEAD_DIM)
    return jnp.where(keep, q.astype(F32), 0.0)


def _moba_kernel(q_ref, k_ref, v_ref, o_ref, *, seq):
    BS = MOBA_BLOCK
    NB = seq // BS
    pair = pl.program_id(1)
    qb = pl.program_id(2)
    scale = HEAD_DIM ** -0.5
    q = q_ref[...]

    kmean = jnp.sum(k_ref[...].astype(F32).reshape(NB, BS, LANES), axis=1) * (1.0 / BS)
    kmean = jnp.concatenate([kmean, jnp.zeros((LANES - NB, LANES), F32)], axis=0)

    lane = lax.broadcasted_iota(jnp.int32, (1, LANES), 1)
    row = lax.broadcasted_iota(jnp.int32, (BS, BS), 0)
    col = lax.broadcasted_iota(jnp.int32, (BS, BS), 1)
    rel = (row - col).astype(F32)
    blk_row = lax.broadcasted_iota(jnp.int32, (LANES, BS), 0)
    k_own = k_ref[pl.ds(pl.multiple_of(qb * BS, BS), BS), :]
    v_own = v_ref[pl.ds(pl.multiple_of(qb * BS, BS), BS), :]

    qms, sels, slopes, state = [], [], [], []
    for hh in range(PAIR):
        qm = _head_masked(q, hh == 0)
        slope = jnp.where(pair == 0, 2.0 ** (-8.0 * (hh + 1) / N_HEADS),
                          2.0 ** (-8.0 * (PAIR + hh + 1) / N_HEADS)).astype(F32)
        gate = lax.dot_general(qm, kmean, (((1,), (1,)), ((), ())), preferred_element_type=F32,
                               precision=HIGHEST)
        rank = jnp.zeros((BS, LANES), F32)
        for i in range(NB):
            gi = gate[:, i:i + 1]
            beats = jnp.where(gi > gate, 1.0, jnp.where((gi == gate) & (lane > i), 1.0, 0.0))
            rank = rank + jnp.where(i < qb, beats, 0.0)
        sel = jnp.where((rank < MOBA_TOPK) & (lane < qb), 1.0, 0.0).astype(BF16)
        s = _bdot_nt(qm, k_own) * scale - slope * rel
        s = jnp.where(rel >= 0, s, -jnp.inf)
        m = jnp.max(s, axis=-1, keepdims=True)
        p = jnp.exp(s - m)
        l = jnp.sum(p, axis=-1, keepdims=True)
        acc = _bdot(p, v_own)
        qms.append(qm.astype(BF16))
        sels.append(sel)
        slopes.append(slope)
        state.extend([m, l, acc])

    def body(j, carry):
        r0 = pl.multiple_of(j * BS, BS)
        k_j = k_ref[pl.ds(r0, BS), :]
        v_j = v_ref[pl.ds(r0, BS), :]
        onehot = jnp.where(blk_row == j, 1.0, 0.0).astype(BF16)
        dist0 = ((qb - j) * BS).astype(F32)
        new = []
        for hh in range(PAIR):
            m, l, acc = carry[3 * hh:3 * hh + 3]
            chosen = jnp.dot(sels[hh], onehot, preferred_element_type=F32)
            s = _bdot_nt(qms[hh], k_j) * scale - slopes[hh] * (rel + dist0)
            s = jnp.where(chosen > 0.5, s, -jnp.inf)
            m_new = jnp.maximum(m, jnp.max(s, axis=-1, keepdims=True))
            alpha = jnp.exp(m - m_new)
            p = jnp.exp(s - m_new)
            l = l * alpha + jnp.sum(p, axis=-1, keepdims=True)
            acc = acc * alpha + _bdot(p, v_j)
            new.extend([m_new, l, acc])
        return tuple(new)

    state = lax.fori_loop(0, qb, body, tuple(state))
    o0 = state[2] / state[1]
    o1 = state[5] / state[4]
    o_ref[...] = jnp.where(lane < HEAD_DIM, o0, o1).astype(o_ref.dtype)


def _moba(pa, batch, seq):
    BS = MOBA_BLOCK
    nq = seq // BS
    npair = N_HEADS // PAIR
    kern = functools.partial(_moba_kernel, seq=seq)
    return pl.pallas_call(
        kern,
        grid=(batch, npair, nq),
        in_specs=[pl.BlockSpec((BS, LANES), lambda b, p, i: (b * nq + i, p)),
                  pl.BlockSpec((seq, LANES), lambda b, p, i: (b, npair + p)),
                  pl.BlockSpec((seq, LANES), lambda b, p, i: (b, 2 * npair + p))],
        out_specs=pl.BlockSpec((BS, LANES), lambda b, p, i: (b * nq + i, p)),
        out_shape=jax.ShapeDtypeStruct((batch * seq, BRANCH_WIDTH), BF16),
        compiler_params=_params(("parallel", "parallel", "arbitrary")),
        name="moba",
    )(pa, pa, pa)


def _sb_kernel(q_ref, k_ref, v_ref, o_ref):
    TQ = SB_TILE
    qb = pl.program_id(2)
    scale = HEAD_DIM ** -0.5
    q = q_ref[...]
    row = lax.broadcasted_iota(jnp.int32, (TQ, TQ), 0)
    col = lax.broadcasted_iota(jnp.int32, (TQ, TQ), 1)
    below = row > col
    later = jnp.where(below, 1.0, 0.0).astype(BF16)
    lane = lax.broadcasted_iota(jnp.int32, (1, LANES), 1)
    qms = [_head_masked(q, hh == 0).astype(BF16) for hh in range(PAIR)]

    def body(i, carry):
        kb = qb - i
        r0 = pl.multiple_of(kb * TQ, TQ)
        k_j = k_ref[pl.ds(r0, TQ), :]
        v_j = v_ref[pl.ds(r0, TQ), :]
        causal = jnp.logical_or(below, kb < qb)
        new = []
        for hh in range(PAIR):
            tail, acc = carry[2 * hh:2 * hh + 2]
            z = _bdot_nt(qms[hh], k_j) * scale
            log_1m = jnp.where(causal, -_softplus(z), 0.0)
            hi = log_1m.astype(BF16)
            lo = (log_1m - hi.astype(F32)).astype(BF16)
            after = (jnp.dot(hi, later, preferred_element_type=F32)
                     + jnp.dot(lo, later, preferred_element_type=F32))
            log_w = z + log_1m + after + tail
            w = jnp.where(causal, jnp.exp(log_w), 0.0)
            acc = acc + _bdot(w, v_j)
            tail = tail + jnp.sum(log_1m, axis=-1, keepdims=True)
            new.extend([tail, acc])
        return tuple(new)

    init = []
    for hh in range(PAIR):
        init.extend([jnp.zeros((TQ, 1), F32), jnp.zeros((TQ, LANES), F32)])
    out = lax.fori_loop(0, qb + 1, body, tuple(init))
    o_ref[...] = jnp.where(lane < HEAD_DIM, out[1], out[3]).astype(o_ref.dtype)


def _sb(pa, batch, seq):
    TQ = SB_TILE
    nq = seq // TQ
    npair = N_HEADS // PAIR
    base = 3 * npair
    return pl.pallas_call(
        _sb_kernel,
        grid=(batch, npair, nq),
        in_specs=[pl.BlockSpec((TQ, LANES), lambda b, p, i: (b * nq + i, base + p)),
                  pl.BlockSpec((seq, LANES), lambda b, p, i: (b, base + npair + p)),
                  pl.BlockSpec((seq, LANES), lambda b, p, i: (b, base + 2 * npair + p))],
        out_specs=pl.BlockSpec((TQ, LANES), lambda b, p, i: (b * nq + i, p)),
        out_shape=jax.ShapeDtypeStruct((batch * seq, BRANCH_WIDTH), BF16),
        compiler_params=_params(("parallel", "parallel", "arbitrary")),
        name="sb",
    )(pa, pa, pa)


def _merge_kernel(x_ref, ya_ref, yb_ref, yc_ref, yd_ref, gpre_ref, wgate_ref, wbr_ref, wout_ref, gpost_ref, o_ref):
    x = x_ref[...]
    h = _rms(x, gpre_ref[...]).astype(BF16)
    merged = None
    for g, y_ref in enumerate((ya_ref, yb_ref, yc_ref, yd_ref)):
        gate = _sigmoid(jnp.dot(h, wgate_ref[g], preferred_element_type=F32))
        term = gate * jnp.dot(y_ref[...], wbr_ref[g], preferred_element_type=F32)
        merged = term if merged is None else merged + term
    mix = jnp.dot(merged.astype(BF16), wout_ref[...], preferred_element_type=F32)
    o_ref[...] = x + _rms(mix, gpost_ref[...])


def _merge(x, ya, yb, yc, yd, gpre, wgate, wbr, wout, gpost):
    n = x.shape[0]
    tm = TOKEN_TILE
    row = lambda w: pl.BlockSpec((tm, w), lambda i: (i, 0))
    return pl.pallas_call(
        _merge_kernel,
        grid=(n // tm,),
        in_specs=[row(D_MODEL)] + [row(BRANCH_WIDTH)] * N_BRANCH
        + [_const_spec(gpre.shape), _const_spec(wgate.shape), _const_spec(wbr.shape),
           _const_spec(wout.shape), _const_spec(gpost.shape)],
        out_specs=row(D_MODEL),
        out_shape=jax.ShapeDtypeStruct((n, D_MODEL), F32),
        compiler_params=_params(("parallel",)),
        name="merge",
    )(x, ya, yb, yc, yd, gpre, wgate, wbr, wout, gpost)


def _ffn_kernel(x_ref, gpre_ref, wup_ref, wdown_ref, gpost_ref, o_ref):
    x = x_ref[...]
    h = _rms(x, gpre_ref[...]).astype(BF16)
    f = None
    for c in range(D_FF // FF_CHUNK):
        u = jnp.dot(h, wup_ref[:, c * FF_CHUNK:(c + 1) * FF_CHUNK], preferred_element_type=F32)
        a = jnp.square(jnp.maximum(u, 0.0)).astype(BF16)
        t = jnp.dot(a, wdown_ref[c * FF_CHUNK:(c + 1) * FF_CHUNK, :], preferred_element_type=F32)
        f = t if f is None else f + t
    o_ref[...] = x + _rms(f, gpost_ref[...])


def _ffn(x, gpre, wup, wdown, gpost):
    n = x.shape[0]
    tm = TOKEN_TILE
    row = pl.BlockSpec((tm, D_MODEL), lambda i: (i, 0))
    return pl.pallas_call(
        _ffn_kernel,
        grid=(n // tm,),
        in_specs=[row, _const_spec(gpre.shape), _const_spec(wup.shape), _const_spec(wdown.shape),
                  _const_spec(gpost.shape)],
        out_specs=row,
        out_shape=jax.ShapeDtypeStruct((n, D_MODEL), F32),
        compiler_params=_params(("parallel",)),
        name="ffn",
    )(x, gpre, wup, wdown, gpost)


def _pad_lanes(a, width):
    return jnp.pad(a, ((0, 0), (0, width - a.shape[1])))


def _layer_weights(w_in_l):
    W = BRANCH_WIDTH
    gdn_in = 4 * W + 2 * N_HEADS
    off_moba = gdn_in
    off_sb = off_moba + 3 * W
    off_ssd = off_sb + 3 * W
    wg = jnp.concatenate([w_in_l[:, :4 * W], _pad_lanes(w_in_l[:, 4 * W:gdn_in], SMALL_PAD)], axis=1)
    wa = w_in_l[:, off_moba:off_ssd]
    ssd = w_in_l[:, off_ssd:]
    ws = jnp.concatenate([ssd[:, W:W + SSD_CONV_DIM], ssd[:, :W],
                          _pad_lanes(ssd[:, W + SSD_CONV_DIM:], SMALL_PAD)], axis=1)
    return wg.astype(BF16), wa.astype(BF16), ws.astype(BF16)


def kernel(x, norm_mix_pre, norm_mix_post, norm_ffn_pre, norm_ffn_post, w_in, gdn_conv, gdn_a_log, gdn_dt_bias,
           gdn_norm, ssd_conv, ssd_conv_bias, ssd_a_log, ssd_dt_bias, ssd_d, ssd_norm, w_gate, w_branch, w_out,
           w_up, w_down):
    batch, seq, d = x.shape
    assert d == D_MODEL and seq % MOBA_BLOCK == 0 and (batch * seq) % TOKEN_TILE == 0
    depth = w_in.shape[0]
    xf = x.reshape(batch * seq, d).astype(F32)
    row = lambda a: a.reshape(1, -1).astype(F32)
    for l in range(depth):
        wg, wa, ws = _layer_weights(w_in[l])
        pg, pa, ps = _in_proj(xf, row(norm_mix_pre[l]), wg, wa, ws)
        gdn_a = _pad_lanes(jnp.concatenate([jnp.zeros((N_HEADS,), F32), gdn_a_log[l]]).reshape(1, -1), SMALL_PAD)
        gdn_b = _pad_lanes(jnp.concatenate([jnp.zeros((N_HEADS,), F32), gdn_dt_bias[l]]).reshape(1, -1), SMALL_PAD)
        ya = _gdn(pg, gdn_conv[l].astype(F32), gdn_a, gdn_b, row(gdn_norm[l]), batch, seq)
        yb = _moba(pa, batch, seq)
        yc = _sb(pa, batch, seq)
        yd = _ssd(ps, ssd_conv[l].astype(F32), row(ssd_conv_bias[l]),
                  _pad_lanes(row(ssd_a_log[l]), SMALL_PAD), _pad_lanes(row(ssd_dt_bias[l]), SMALL_PAD),
                  row(jnp.repeat(ssd_d[l], HEAD_DIM)), row(ssd_norm[l]), batch, seq)
        xf = _merge(xf, ya, yb, yc, yd, row(norm_mix_pre[l]), w_gate[l].astype(BF16), w_branch[l].astype(BF16),
                    w_out[l].astype(BF16), row(norm_mix_post[l]))
        xf = _ffn(xf, row(norm_ffn_pre[l]), w_up[l].astype(BF16), w_down[l].astype(BF16), row(norm_ffn_post[l]))
    return xf.reshape(batch, seq, d).astype(x.dtype)
```

```python
import functools

import jax
import jax.numpy as jnp
from jax import lax
from jax.experimental import pallas as pl
from jax.experimental.pallas import tpu as pltpu

F32 = jnp.float32
BF16 = jnp.bfloat16
HIGHEST = lax.Precision.HIGHEST

D_MODEL = 1024
N_BRANCH = 4
BRANCH_WIDTH = 256
HEAD_DIM = 64
N_HEADS = 4
CONV_WIDTH = 4
GDN_CHUNK = 64
MOBA_BLOCK = 256
MOBA_TOPK = 3
SSD_STATE = 128
SSD_GROUPS = 2
SSD_CHUNK = 128
D_FF = 4 * D_MODEL
EPS = 1e-6

LANES = 128
SUBLANES = 8
PAIR = LANES // HEAD_DIM
SMALL_PAD = LANES
CONV_HALO = 8
TOKEN_TILE = 512
FF_CHUNK = 1024
SB_TILE = 256
GDN_UNROLL = 2
VMEM_LIMIT = 56 * 1024 * 1024

GDN_W = 3 * BRANCH_WIDTH + BRANCH_WIDTH + SMALL_PAD
ATT_W = 6 * BRANCH_WIDTH
SSD_CONV_DIM = BRANCH_WIDTH + 2 * SSD_GROUPS * SSD_STATE
SSD_W = SSD_CONV_DIM + BRANCH_WIDTH + SMALL_PAD


def _dot(a, b):
    return jnp.dot(a, b, preferred_element_type=F32)


def _bdot(a, b):
    return _dot(a.astype(BF16), b.astype(BF16))


def _bdot_nt(a, b):
    return lax.dot_general(a.astype(BF16), b.astype(BF16), (((1,), (1,)), ((), ())),
                           preferred_element_type=F32)


def _hdot(a, b):
    return jnp.dot(a, b, preferred_element_type=F32, precision=HIGHEST)


def _split(x, pieces):
    out = []
    for _ in range(pieces - 1):
        p = x.astype(BF16)
        out.append(p)
        x = x - p.astype(F32)
    out.append(x.astype(BF16))
    return out


def _dot_data_const(x, m, pieces):
    acc = None
    for p in _split(x, pieces):
        t = _dot(p, m)
        acc = t if acc is None else acc + t
    return acc


def _dot_const_data(m, x, pieces):
    acc = None
    for p in _split(x, pieces):
        t = _dot(m, p)
        acc = t if acc is None else acc + t
    return acc


def _sigmoid(x):
    return 1.0 / (1.0 + jnp.exp(-x))


def _silu(x):
    return x * _sigmoid(x)


def _softplus(x):
    return jnp.maximum(x, 0.0) + jnp.log(1.0 + jnp.exp(-jnp.abs(x)))


def _rms(x, gain):
    return x * lax.rsqrt(jnp.mean(x * x, axis=-1, keepdims=True) + EPS) * gain


def _params(sem):
    return pltpu.CompilerParams(dimension_semantics=sem, vmem_limit_bytes=VMEM_LIMIT)


def _const_spec(shape):
    nd = len(shape)
    return pl.BlockSpec(shape, lambda *_: (0,) * nd, pipeline_mode=pl.Buffered(1))


def _in_proj_kernel(x_ref, gain_ref, wg_ref, wa_ref, ws_ref, og_ref, oa_ref, os_ref):
    h = _rms(x_ref[...], gain_ref[...]).astype(BF16)
    og_ref[...] = _dot(h, wg_ref[...])
    oa_ref[...] = _dot(h, wa_ref[...]).astype(BF16)
    os_ref[...] = _dot(h, ws_ref[...])


def _in_proj(x, gain, wg, wa, ws):
    n = x.shape[0]
    tm = TOKEN_TILE
    row = lambda w: pl.BlockSpec((tm, w), lambda i: (i, 0))
    return pl.pallas_call(
        _in_proj_kernel,
        grid=(n // tm,),
        in_specs=[row(D_MODEL), _const_spec((1, D_MODEL)), _const_spec(wg.shape),
                  _const_spec(wa.shape), _const_spec(ws.shape)],
        out_specs=[row(GDN_W), row(ATT_W), row(SSD_W)],
        out_shape=[jax.ShapeDtypeStruct((n, GDN_W), F32), jax.ShapeDtypeStruct((n, ATT_W), BF16),
                   jax.ShapeDtypeStruct((n, SSD_W), F32)],
        compiler_params=_params(("parallel",)),
        name="in_proj",
    )(x, gain, wg, wa, ws)


def _conv_silu_into(src_ref, w_ref, bias, dst_ref, seq, chunk):
    w = w_ref[...]
    taps = [w[k:k + 1, :] for k in range(CONV_WIDTH)]

    def finish(acc):
        if bias is not None:
            acc = acc + bias
        return _silu(acc)

    blk = src_ref[0:chunk, :]
    rows = lax.broadcasted_iota(jnp.int32, (chunk, 1), 0)
    acc = taps[CONV_WIDTH - 1] * blk
    for k in range(CONV_WIDTH - 1):
        shift = CONV_WIDTH - 1 - k
        acc = acc + taps[k] * jnp.where(rows >= shift, pltpu.roll(blk, shift, 0), 0.0)
    dst_ref[0:chunk, :] = finish(acc)

    def body(c, carry):
        r0 = pl.multiple_of(c * chunk, chunk)
        blk = src_ref[pl.ds(r0 - CONV_HALO, chunk + CONV_HALO), :]
        acc = taps[CONV_WIDTH - 1] * blk[CONV_HALO:]
        for k in range(CONV_WIDTH - 1):
            acc = acc + taps[k] * pltpu.roll(blk, CONV_WIDTH - 1 - k, 0)[CONV_HALO:]
        dst_ref[pl.ds(r0, chunk), :] = finish(acc)
        return carry

    lax.fori_loop(1, seq // chunk, body, 0)


def _block_diag(x, mask):
    return jnp.where(mask, jnp.concatenate([x] * N_HEADS, axis=0), jnp.zeros((), x.dtype))


def _dot3_bd(a, b, mask):
    ah, al = _split(a, 2)
    bh, bl = _split(b, 2)
    bdh = _block_diag(bh, mask)
    return _dot(ah, bdh) + _dot(al, bdh) + _dot(ah, _block_diag(bl, mask))


def _gdn_kernel(qkv_ref, z_ref, bg_ref, convw_ref, alog_ref, dtb_ref, norm_ref, o_ref,
                conv_ref, u_ref, wq_ref, aqk_ref, kdt_ref, gl_ref, s_ref, *, seq):
    C = GDN_CHUNK
    W = BRANCH_WIDTH
    NC = seq // C
    _conv_silu_into(qkv_ref, convw_ref, None, conv_ref, seq, 4 * C)

    def iota(shape, dim):
        return lax.broadcasted_iota(jnp.int32, shape, dim)

    head_of_lane = iota((1, W), 1) // HEAD_DIM
    same_head = (iota((W, W), 0) // HEAD_DIM) == (iota((W, W), 1) // HEAD_DIM)
    same_head2 = jnp.concatenate([same_head, same_head], axis=1)
    seg_ones = jnp.where(same_head, 1.0, 0.0).astype(BF16)
    r_c = iota((C, W), 0)
    j_c = iota((C, W), 1) % HEAD_DIM
    incl = r_c >= j_c
    strict = r_c > j_c
    eye = jnp.where(r_c == j_c, 1.0, 0.0)
    tril = jnp.where(iota((C, C), 0) >= iota((C, C), 1), 1.0, 0.0).astype(BF16)
    ones_c = jnp.ones((C, C), BF16)
    lane = iota((1, LANES), 1)
    ex_r = iota((LANES, 2 * W), 0)
    ex_c = iota((LANES, 2 * W), 1)
    expand = jnp.where(ex_r == jnp.where(ex_c < W, ex_c // HEAD_DIM, N_HEADS + (ex_c - W) // HEAD_DIM),
                       1.0, 0.0).astype(BF16)
    neg_a = -jnp.exp(alog_ref[...])
    dtb = dtb_ref[...]
    gain = norm_ref[...]
    scale = HEAD_DIM ** -0.5

    def prepare(i, carry):
        cs = [i * GDN_UNROLL + n for n in range(GDN_UNROLL)]
        r0s = [pl.multiple_of(c * C, C) for c in cs]
        each = lambda f, *xs: [f(*a) for a in zip(*xs)]
        qkv = [conv_ref[pl.ds(r0, C), :] for r0 in r0s]
        q = [a[:, :W] for a in qkv]
        k = [a[:, W:2 * W] for a in qkv]
        v = [a[:, 2 * W:] for a in qkv]
        ss = each(lambda q_, k_: _dot_data_const(jnp.concatenate([q_ * q_, k_ * k_], axis=0), seg_ones, 2), q, k)
        qn = each(lambda q_, s_: q_ * lax.rsqrt(s_[:C] + EPS) * scale, q, ss)
        kn = each(lambda k_, s_: k_ * lax.rsqrt(s_[C:] + EPS), k, ss)
        bg = [bg_ref[pl.ds(r0, C), :] for r0 in r0s]
        gc = each(lambda b_: _dot_const_data(tril, neg_a * _softplus(b_ + dtb), 3), bg)
        ex = each(lambda b_, g_: _dot_data_const(jnp.where(lane < N_HEADS, _sigmoid(b_), g_), expand, 3), bg, gc)
        beta = [a[:, :W] for a in ex]
        gcc = [a[:, W:] for a in ex]
        gcr = each(lambda g_: _dot_const_data(ones_c, g_ * eye, 3), gcc)
        dm = each(lambda a_, b_: jnp.exp(jnp.where(incl, a_ - b_, -jnp.inf)), gcc, gcr)
        kb = each(lambda a_, b_: a_ * b_, kn, beta)
        vb = each(lambda a_, b_: a_ * b_, v, beta)
        qk = each(lambda q_, kb_, kn_: lax.dot_general(
            jnp.concatenate([q_, kb_], axis=0).astype(BF16), _block_diag(kn_.astype(BF16), same_head),
            (((1,), (1,)), ((), ())), preferred_element_type=F32), qn, kb, kn)
        a_qk = each(lambda x_, d_: x_[:C] * d_, qk, dm)
        a = each(lambda x_, d_: jnp.where(strict, x_[C:] * d_, 0.0), qk, dm)
        p = [eye - a_ for a_ in a]
        b = each(lambda a_: _dot3_bd(a_, a_, same_head), a)
        power = 2
        while 2 * power < C:
            pb = each(lambda p_, b_: _dot3_bd(jnp.concatenate([p_, b_], axis=0), b_, same_head), p, b)
            p = each(lambda p_, x_: p_ + x_[:C], p, pb)
            b = [x_[C:] for x_ in pb]
            power *= 2
        t_inv = each(lambda p_, b_: p_ + _dot3_bd(p_, b_, same_head), p, b)
        e_gc = [jnp.exp(g_) for g_ in gcc]
        uw = each(lambda t_, vb_, kb_, e_: _dot3_bd(t_, jnp.concatenate([vb_, kb_ * e_], axis=1), same_head2),
                  t_inv, vb, kb, e_gc)
        for n, c in enumerate(cs):
            r0 = r0s[n]
            g_last = gcc[n][C - 1:C, :]
            u_ref[pl.ds(r0, C), :] = uw[n][:, :W]
            wq_ref[pl.ds(pl.multiple_of(c * 2 * C, 2 * C), 2 * C), :] = jnp.concatenate(
                [uw[n][:, W:], qn[n] * e_gc[n]], axis=0).astype(BF16)
            aqk_ref[pl.ds(r0, C), :] = a_qk[n].astype(BF16)
            k_dec = jnp.concatenate([kn[n] * jnp.exp(g_last - gcc[n]), jnp.zeros((LANES - C, W), F32)], axis=0)
            kdt_ref[pl.ds(pl.multiple_of(c * W, W), W), :] = jnp.concatenate(
                [k_dec[:, :LANES].T, k_dec[:, LANES:].T], axis=0).astype(BF16)
            gl_ref[c] = jnp.broadcast_to(jnp.exp(g_last), (SUBLANES, W))
        return carry

    lax.fori_loop(0, NC // GDN_UNROLL, prepare, 0)

    s_ref[...] = jnp.zeros_like(s_ref)

    def recur(c, carry):
        r0 = pl.multiple_of(c * C, C)
        s = s_ref[...]
        ws_qs = _dot(wq_ref[pl.ds(pl.multiple_of(c * 2 * C, 2 * C), 2 * C), :], s.astype(BF16))
        v_new = (u_ref[pl.ds(r0, C), :] - ws_qs[:C]).astype(BF16)
        upd = _dot(kdt_ref[pl.ds(pl.multiple_of(c * W, W), W), :],
                   jnp.concatenate([v_new, jnp.zeros((LANES - C, W), BF16)], axis=0))
        s_ref[...] = s * gl_ref[c][0:1, :] + jnp.where(same_head, upd, 0.0)
        o = ws_qs[C:] + _dot(aqk_ref[pl.ds(r0, C), :], _block_diag(v_new, same_head))
        ms = _dot_data_const(o * o, seg_ones, 2) * (1.0 / HEAD_DIM)
        y = o * lax.rsqrt(ms + EPS) * gain * _silu(z_ref[pl.ds(r0, C), :])
        o_ref[pl.ds(r0, C), :] = y.astype(o_ref.dtype)
        return carry

    lax.fori_loop(0, NC, recur, 0)
    del head_of_lane


def _gdn(pg, conv_w, a_row, dtb_row, norm_row, batch, seq):
    kern = functools.partial(_gdn_kernel, seq=seq)
    C = GDN_CHUNK
    nc = seq // C
    return pl.pallas_call(
        kern,
        grid=(batch,),
        in_specs=[pl.BlockSpec((seq, 3 * BRANCH_WIDTH), lambda b: (b, 0)),
                  pl.BlockSpec((seq, BRANCH_WIDTH), lambda b: (b, 3)),
                  pl.BlockSpec((seq, SMALL_PAD), lambda b: (b, 4 * BRANCH_WIDTH // SMALL_PAD)),
                  _const_spec(conv_w.shape), _const_spec(a_row.shape), _const_spec(dtb_row.shape),
                  _const_spec(norm_row.shape)],
        out_specs=pl.BlockSpec((seq, BRANCH_WIDTH), lambda b: (b, 0)),
        out_shape=jax.ShapeDtypeStruct((batch * seq, BRANCH_WIDTH), BF16),
        scratch_shapes=[pltpu.VMEM((seq, 3 * BRANCH_WIDTH), F32),
                        pltpu.VMEM((seq, BRANCH_WIDTH), F32),
                        pltpu.VMEM((2 * seq, BRANCH_WIDTH), BF16),
                        pltpu.VMEM((seq, BRANCH_WIDTH), BF16),
                        pltpu.VMEM((nc * BRANCH_WIDTH, LANES), BF16),
                        pltpu.VMEM((nc, SUBLANES, BRANCH_WIDTH), F32),
                        pltpu.VMEM((BRANCH_WIDTH, BRANCH_WIDTH), F32)],
        compiler_params=_params(("parallel",)),
        name="gdn",
    )(pg, pg, pg, conv_w, a_row, dtb_row, norm_row)


def _ssd_kernel(xbc_ref, z_ref, dt_ref, convw_ref, convb_ref, alog_ref, dtb_ref, dskip_ref, norm_ref,
                o_ref, conv_ref, s_ref, *, seq):
    L = SSD_CHUNK
    W = BRANCH_WIDTH
    GN = SSD_GROUPS * SSD_STATE
    _conv_silu_into(xbc_ref, convw_ref, convb_ref[...], conv_ref, seq, L)
    s_ref[...] = jnp.zeros_like(s_ref)

    ri = lax.broadcasted_iota(jnp.int32, (L, L), 0)
    ci = lax.broadcasted_iota(jnp.int32, (L, L), 1)
    incl = ri >= ci
    tril = jnp.where(incl, 1.0, 0.0)
    first = lax.broadcasted_iota(jnp.int32, (1, LANES), 1) < HEAD_DIM
    neg_a = -jnp.exp(alog_ref[...])
    dtb = dtb_ref[...]
    dskip = dskip_ref[...]
    gain = norm_ref[...]

    def body(c, carry):
        r0 = pl.multiple_of(c * L, L)
        xbc = conv_ref[pl.ds(r0, L), :]
        zt = z_ref[pl.ds(r0, L), :]
        dt = _softplus(dt_ref[pl.ds(r0, L), :] + dtb)
        acs = _hdot(tril, dt * neg_a)
        acs_rows = acs.T
        a_last = acs[L - 1:L, :]
        e_acs = jnp.exp(acs)
        dt_rem = dt * jnp.exp(a_last - acs)
        e_last = jnp.exp(a_last)
        ys = []
        for g in range(SSD_GROUPS):
            h0, h1 = PAIR * g, PAIR * g + 1
            x2 = xbc[:, g * LANES:(g + 1) * LANES]
            bm = xbc[:, W + g * SSD_STATE:W + (g + 1) * SSD_STATE]
            cm = xbc[:, W + GN + g * SSD_STATE:W + GN + (g + 1) * SSD_STATE]
            cb = _bdot_nt(cm, bm)
            xdt = x2 * jnp.where(first, dt[:, h0:h0 + 1], dt[:, h1:h1 + 1])
            yd = []
            for h in (h0, h1):
                lmat = jnp.exp(jnp.where(incl, acs[:, h:h + 1] - acs_rows[h:h + 1, :], -jnp.inf))
                yd.append(_bdot(cb * lmat, xdt))
            y_diag = jnp.where(first, yd[0], yd[1])
            s = s_ref[g]
            y_off = _bdot(cm, s) * jnp.where(first, e_acs[:, h0:h0 + 1], e_acs[:, h1:h1 + 1])
            x_rem = x2 * jnp.where(first, dt_rem[:, h0:h0 + 1], dt_rem[:, h1:h1 + 1])
            s_ref[g] = (s * jnp.where(first, e_last[:, h0:h0 + 1], e_last[:, h1:h1 + 1])
                        + _bdot(bm.T, x_rem))
            ys.append(y_diag + y_off)
        y = jnp.concatenate(ys, axis=1) + dskip * xbc[:, :W]
        o_ref[pl.ds(r0, L), :] = _rms(y * _silu(zt), gain).astype(o_ref.dtype)
        return carry

    lax.fori_loop(0, seq // L, body, 0)


def _ssd(ps, conv_w, conv_b, a_row, dtb_row, dskip_row, norm_row, batch, seq):
    kern = functools.partial(_ssd_kernel, seq=seq)
    return pl.pallas_call(
        kern,
        grid=(batch,),
        in_specs=[pl.BlockSpec((seq, SSD_CONV_DIM), lambda b: (b, 0)),
                  pl.BlockSpec((seq, BRANCH_WIDTH), lambda b: (b, SSD_CONV_DIM // BRANCH_WIDTH)),
                  pl.BlockSpec((seq, SMALL_PAD), lambda b: (b, (SSD_CONV_DIM + BRANCH_WIDTH) // SMALL_PAD)),
                  _const_spec(conv_w.shape), _const_spec(conv_b.shape), _const_spec(a_row.shape),
                  _const_spec(dtb_row.shape), _const_spec(dskip_row.shape), _const_spec(norm_row.shape)],
        out_specs=pl.BlockSpec((seq, BRANCH_WIDTH), lambda b: (b, 0)),
        out_shape=jax.ShapeDtypeStruct((batch * seq, BRANCH_WIDTH), BF16),
        scratch_shapes=[pltpu.VMEM((seq, SSD_CONV_DIM), F32),
                        pltpu.VMEM((SSD_GROUPS, SSD_STATE, LANES), F32)],
        compiler_params=_params(("parallel",)),
        name="ssd",
    )(ps, ps, ps, conv_w, conv_b, a_row, dtb_row, dskip_row, norm_row)


def _head_masked(q, first_half):
    lane = lax.broadcasted_iota(jnp.int32, (1, LANES), 1)
    keep = (lane < HEAD_DIM) if first_half else (lane >= HEAD_DIM)
    return jnp.where(keep, q.astype(F32), 0.0)


def _moba_kernel(q_ref, k_ref, v_ref, o_ref, kmean_ref, *, seq):
    BS = MOBA_BLOCK
    NB = seq // BS
    W = BRANCH_WIDTH
    qb = pl.program_id(1)
    scale = HEAD_DIM ** -0.5
    q = q_ref[...]

    @pl.when(qb == 0)
    def _():
        kmean = jnp.sum(k_ref[...].astype(F32).reshape(NB, BS, W), axis=1) * (1.0 / BS)
        kmean_ref[...] = jnp.concatenate([kmean, jnp.zeros((LANES - NB, W), F32)], axis=0)

    lane = lax.broadcasted_iota(jnp.int32, (1, LANES), 1)
    rel = (lax.broadcasted_iota(jnp.int32, (BS, BS), 0)
           - lax.broadcasted_iota(jnp.int32, (BS, BS), 1)).astype(F32)
    blk_row = lax.broadcasted_iota(jnp.int32, (LANES, BS), 0)
    blk_id = lax.broadcasted_iota(jnp.int32, (SUBLANES, BS), 0)
    heads = range(N_HEADS)
    pair_of = [h // PAIR for h in heads]
    lanes_of = lambda x, p: x[:, p * LANES:(p + 1) * LANES]
    slopes = [2.0 ** (-8.0 * (h + 1) / N_HEADS) for h in heads]

    qf = [_head_masked(lanes_of(q, pair_of[h]), h % PAIR == 0) for h in heads]
    gate = [lax.dot_general(lanes_of(kmean_ref[...], pair_of[h]), qf[h], (((1,), (1,)), ((), ())),
                            preferred_element_type=F32, precision=HIGHEST)[:NB] for h in heads]
    sels = []
    for h in heads:
        rank = jnp.zeros((NB, BS), F32)
        for i in range(NB):
            gi = gate[h][i:i + 1, :]
            beats = jnp.where(gi > gate[h], 1.0, jnp.where((gi == gate[h]) & (blk_id > i), 1.0, 0.0))
            rank = rank + jnp.where(i < qb, beats, 0.0)
        sel_t = jnp.where((rank < MOBA_TOPK) & (blk_id < qb), 1.0, 0.0)
        sel_t = jnp.concatenate([sel_t, jnp.zeros((LANES - NB, BS), F32)], axis=0)
        sels.append(jnp.concatenate([sel_t[:, :LANES].T, sel_t[:, LANES:].T], axis=0).astype(BF16))
    qms = [(qf[h] * scale).astype(BF16) for h in heads]

    own0 = pl.multiple_of(qb * BS, BS)
    k_own = k_ref[pl.ds(own0, BS), :]
    v_own = v_ref[pl.ds(own0, BS), :]
    s = [_bdot_nt(qms[h], lanes_of(k_own, pair_of[h])) for h in heads]
    s = [jnp.where(rel >= 0, s[h] - slopes[h] * rel, -jnp.inf) for h in heads]
    m = [jnp.max(s_, axis=-1, keepdims=True) for s_ in s]
    p = [jnp.exp(s[h] - m[h]) for h in heads]
    l = [jnp.sum(p_, axis=-1, keepdims=True) for p_ in p]
    acc = [_bdot(p[h], lanes_of(v_own, pair_of[h])) for h in heads]

    def body(j, carry):
        r0 = pl.multiple_of(j * BS, BS)
        k_j = k_ref[pl.ds(r0, BS), :]
        v_j = v_ref[pl.ds(r0, BS), :]
        onehot = jnp.where(blk_row == j, 1.0, 0.0).astype(BF16)
        dist = rel + ((qb - j) * BS).astype(F32)
        m, l, acc = carry[:N_HEADS], carry[N_HEADS:2 * N_HEADS], carry[2 * N_HEADS:]
        s = [_bdot_nt(qms[h], lanes_of(k_j, pair_of[h])) for h in heads]
        chosen = [_dot(sels[h], onehot) for h in heads]
        s = [jnp.where(chosen[h] > 0.5, s[h] - slopes[h] * dist, -jnp.inf) for h in heads]
        m_new = [jnp.maximum(m[h], jnp.max(s[h], axis=-1, keepdims=True)) for h in heads]
        alpha = [jnp.exp(m[h] - m_new[h]) for h in heads]
        p = [jnp.exp(s[h] - m_new[h]) for h in heads]
        l = [l[h] * alpha[h] + jnp.sum(p[h], axis=-1, keepdims=True) for h in heads]
        acc = [acc[h] * alpha[h] + _bdot(p[h], lanes_of(v_j, pair_of[h])) for h in heads]
        return tuple(m_new) + tuple(l) + tuple(acc)

    out = lax.fori_loop(0, qb, body, tuple(m) + tuple(l) + tuple(acc))
    l, acc = out[N_HEADS:2 * N_HEADS], out[2 * N_HEADS:]
    o = [acc[h] / l[h] for h in heads]
    o_ref[...] = jnp.concatenate(
        [jnp.where(lane < HEAD_DIM, o[PAIR * p_], o[PAIR * p_ + 1]) for p_ in range(N_HEADS // PAIR)],
        axis=1).astype(o_ref.dtype)


def _moba(pa, batch, seq):
    BS = MOBA_BLOCK
    nq = seq // BS
    assert nq == SUBLANES
    W = BRANCH_WIDTH
    kern = functools.partial(_moba_kernel, seq=seq)
    return pl.pallas_call(
        kern,
        grid=(batch, nq),
        in_specs=[pl.BlockSpec((BS, W), lambda b, i: (b * nq + i, 0)),
                  pl.BlockSpec((seq, W), lambda b, i: (b, 1)),
                  pl.BlockSpec((seq, W), lambda b, i: (b, 2))],
        out_specs=pl.BlockSpec((BS, W), lambda b, i: (b * nq + i, 0)),
        out_shape=jax.ShapeDtypeStruct((batch * seq, W), BF16),
        scratch_shapes=[pltpu.VMEM((LANES, W), F32)],
        compiler_params=_params(("parallel", "arbitrary")),
        name="moba",
    )(pa, pa, pa)


def _sb_kernel(q_ref, k_ref, v_ref, o_ref):
    TQ = SB_TILE
    qb = pl.program_id(1)
    scale = HEAD_DIM ** -0.5
    q = q_ref[...]
    below = lax.broadcasted_iota(jnp.int32, (TQ, TQ), 0) > lax.broadcasted_iota(jnp.int32, (TQ, TQ), 1)
    later = jnp.where(below, 1.0, 0.0).astype(BF16)
    lane = lax.broadcasted_iota(jnp.int32, (1, LANES), 1)
    heads = range(N_HEADS)
    pair_of = [h // PAIR for h in heads]
    lanes_of = lambda x, p: x[:, p * LANES:(p + 1) * LANES]
    qms = [(_head_masked(lanes_of(q, pair_of[h]), h % PAIR == 0) * scale).astype(BF16) for h in heads]

    def tile(kb, carry, diagonal):
        r0 = pl.multiple_of(kb * TQ, TQ)
        k_j = k_ref[pl.ds(r0, TQ), :]
        v_j = v_ref[pl.ds(r0, TQ), :]
        tails, accs = carry[:N_HEADS], carry[N_HEADS:]
        z = [_bdot_nt(qms[h], lanes_of(k_j, pair_of[h])) for h in heads]
        log_1m = [-_softplus(z_) for z_ in z]
        if diagonal:
            log_1m = [jnp.where(below, l_, 0.0) for l_ in log_1m]
        after = [_dot_data_const(l_, later, 2) for l_ in log_1m]
        w = [jnp.exp(z[h] + log_1m[h] + after[h] + tails[h]) for h in heads]
        if diagonal:
            w = [jnp.where(below, w_, 0.0) for w_ in w]
        accs = [accs[h] + _bdot(w[h], lanes_of(v_j, pair_of[h])) for h in heads]
        tails = [tails[h] + jnp.sum(log_1m[h], axis=-1, keepdims=True) for h in heads]
        return tuple(tails) + tuple(accs)

    init = tuple(jnp.zeros((TQ, 1), F32) for _ in heads) + tuple(jnp.zeros((TQ, LANES), F32) for _ in heads)
    carry = tile(qb, init, True)
    out = lax.fori_loop(0, qb, lambda i, c: tile(qb - 1 - i, c, False), carry)
    accs = out[N_HEADS:]
    o_ref[...] = jnp.concatenate(
        [jnp.where(lane < HEAD_DIM, accs[PAIR * p], accs[PAIR * p + 1]) for p in range(N_HEADS // PAIR)],
        axis=1).astype(o_ref.dtype)


def _sb(pa, batch, seq):
    TQ = SB_TILE
    nq = seq // TQ
    W = BRANCH_WIDTH
    return pl.pallas_call(
        _sb_kernel,
        grid=(batch, nq),
        in_specs=[pl.BlockSpec((TQ, W), lambda b, i: (b * nq + i, 3)),
                  pl.BlockSpec((seq, W), lambda b, i: (b, 4)),
                  pl.BlockSpec((seq, W), lambda b, i: (b, 5))],
        out_specs=pl.BlockSpec((TQ, W), lambda b, i: (b * nq + i, 0)),
        out_shape=jax.ShapeDtypeStruct((batch * seq, W), BF16),
        compiler_params=_params(("parallel", "arbitrary")),
        name="sb",
    )(pa, pa, pa)


def _merge_kernel(x_ref, ya_ref, yb_ref, yc_ref, yd_ref, gpre_ref, wgate_ref, wbr_ref, wout_ref, gpost_ref, o_ref):
    x = x_ref[...]
    h = _rms(x, gpre_ref[...]).astype(BF16)
    merged = None
    for g, y_ref in enumerate((ya_ref, yb_ref, yc_ref, yd_ref)):
        gate = _sigmoid(_dot(h, wgate_ref[g]))
        term = gate * _dot(y_ref[...], wbr_ref[g])
        merged = term if merged is None else merged + term
    mix = _dot(merged.astype(BF16), wout_ref[...])
    o_ref[...] = x + _rms(mix, gpost_ref[...])


def _merge(x, ya, yb, yc, yd, gpre, wgate, wbr, wout, gpost):
    n = x.shape[0]
    tm = TOKEN_TILE
    row = lambda w: pl.BlockSpec((tm, w), lambda i: (i, 0))
    return pl.pallas_call(
        _merge_kernel,
        grid=(n // tm,),
        in_specs=[row(D_MODEL)] + [row(BRANCH_WIDTH)] * N_BRANCH
        + [_const_spec(gpre.shape), _const_spec(wgate.shape), _const_spec(wbr.shape),
           _const_spec(wout.shape), _const_spec(gpost.shape)],
        out_specs=row(D_MODEL),
        out_shape=jax.ShapeDtypeStruct((n, D_MODEL), F32),
        compiler_params=_params(("parallel",)),
        name="merge",
    )(x, ya, yb, yc, yd, gpre, wgate, wbr, wout, gpost)


def _ffn_kernel(x_ref, gpre_ref, wup_ref, wdown_ref, gpost_ref, o_ref):
    x = x_ref[...]
    h = _rms(x, gpre_ref[...]).astype(BF16)
    f = None
    for c in range(D_FF // FF_CHUNK):
        u = _dot(h, wup_ref[:, c * FF_CHUNK:(c + 1) * FF_CHUNK])
        a = jnp.square(jnp.maximum(u, 0.0)).astype(BF16)
        t = _dot(a, wdown_ref[c * FF_CHUNK:(c + 1) * FF_CHUNK, :])
        f = t if f is None else f + t
    o_ref[...] = x + _rms(f, gpost_ref[...])


def _ffn(x, gpre, wup, wdown, gpost):
    n = x.shape[0]
    tm = TOKEN_TILE
    row = pl.BlockSpec((tm, D_MODEL), lambda i: (i, 0))
    return pl.pallas_call(
        _ffn_kernel,
        grid=(n // tm,),
        in_specs=[row, _const_spec(gpre.shape), _const_spec(wup.shape), _const_spec(wdown.shape),
                  _const_spec(gpost.shape)],
        out_specs=row,
        out_shape=jax.ShapeDtypeStruct((n, D_MODEL), F32),
        compiler_params=_params(("parallel",)),
        name="ffn",
    )(x, gpre, wup, wdown, gpost)


def _pad_lanes(a, width):
    return jnp.pad(a, ((0, 0), (0, width - a.shape[1])))


def _layer_weights(w_in_l):
    W = BRANCH_WIDTH
    gdn_in = 4 * W + 2 * N_HEADS
    off_moba = gdn_in
    off_sb = off_moba + 3 * W
    off_ssd = off_sb + 3 * W
    wg = jnp.concatenate([w_in_l[:, :4 * W], _pad_lanes(w_in_l[:, 4 * W:gdn_in], SMALL_PAD)], axis=1)
    wa = w_in_l[:, off_moba:off_ssd]
    ssd = w_in_l[:, off_ssd:]
    ws = jnp.concatenate([ssd[:, W:W + SSD_CONV_DIM], ssd[:, :W],
                          _pad_lanes(ssd[:, W + SSD_CONV_DIM:], SMALL_PAD)], axis=1)
    return wg.astype(BF16), wa.astype(BF16), ws.astype(BF16)


def kernel(x, norm_mix_pre, norm_mix_post, norm_ffn_pre, norm_ffn_post, w_in, gdn_conv, gdn_a_log, gdn_dt_bias,
           gdn_norm, ssd_conv, ssd_conv_bias, ssd_a_log, ssd_dt_bias, ssd_d, ssd_norm, w_gate, w_branch, w_out,
           w_up, w_down):
    batch, seq, d = x.shape
    assert d == D_MODEL and seq % MOBA_BLOCK == 0 and (batch * seq) % TOKEN_TILE == 0
    depth = w_in.shape[0]
    xf = x.reshape(batch * seq, d).astype(F32)
    row = lambda a: a.reshape(1, -1).astype(F32)
    for l in range(depth):
        wg, wa, ws = _layer_weights(w_in[l])
        pg, pa, ps = _in_proj(xf, row(norm_mix_pre[l]), wg, wa, ws)
        gdn_a = _pad_lanes(jnp.concatenate([jnp.zeros((N_HEADS,), F32), gdn_a_log[l]]).reshape(1, -1), SMALL_PAD)
        gdn_b = _pad_lanes(jnp.concatenate([jnp.zeros((N_HEADS,), F32), gdn_dt_bias[l]]).reshape(1, -1), SMALL_PAD)
        ya = _gdn(pg, gdn_conv[l].astype(F32), gdn_a, gdn_b, row(jnp.tile(gdn_norm[l], N_HEADS)), batch, seq)
        yb = _moba(pa, batch, seq)
        yc = _sb(pa, batch, seq)
        yd = _ssd(ps, ssd_conv[l].astype(F32), row(ssd_conv_bias[l]),
                  _pad_lanes(row(ssd_a_log[l]), SMALL_PAD), _pad_lanes(row(ssd_dt_bias[l]), SMALL_PAD),
                  row(jnp.repeat(ssd_d[l], HEAD_DIM)), row(ssd_norm[l]), batch, seq)
        xf = _merge(xf, ya, yb, yc, yd, row(norm_mix_pre[l]), w_gate[l].astype(BF16), w_branch[l].astype(BF16),
                    w_out[l].astype(BF16), row(norm_mix_post[l]))
        xf = _ffn(xf, row(norm_ffn_pre[l]), w_up[l].astype(BF16), w_down[l].astype(BF16), row(norm_ffn_post[l]))
    return xf.reshape(batch, seq, d).astype(x.dtype)
```

```python
import functools

import jax
import jax.numpy as jnp
from jax import lax
from jax.experimental import pallas as pl
from jax.experimental.pallas import tpu as pltpu

F32 = jnp.float32
BF16 = jnp.bfloat16
HIGHEST = lax.Precision.HIGHEST

D_MODEL = 1024
N_BRANCH = 4
BRANCH_WIDTH = 256
HEAD_DIM = 64
N_HEADS = 4
CONV_WIDTH = 4
GDN_CHUNK = 64
MOBA_BLOCK = 256
MOBA_TOPK = 3
SSD_STATE = 128
SSD_GROUPS = 2
SSD_CHUNK = 128
D_FF = 4 * D_MODEL
EPS = 1e-6

LANES = 128
SUBLANES = 8
PAIR = LANES // HEAD_DIM
SMALL_PAD = LANES
CONV_HALO = 8
TOKEN_TILE = 512
FF_CHUNK = 1024
SB_TILE = 256
LOG2E = 1.4426950408889634
SB_DEAD_TAIL = 256.0
GDN_UNROLL = 8
VMEM_LIMIT = 56 * 1024 * 1024

GDN_W = 3 * BRANCH_WIDTH + BRANCH_WIDTH + SMALL_PAD
ATT_W = 6 * BRANCH_WIDTH
SSD_CONV_DIM = BRANCH_WIDTH + 2 * SSD_GROUPS * SSD_STATE
SSD_W = SSD_CONV_DIM + BRANCH_WIDTH + SMALL_PAD


def _dot(a, b):
    return jnp.dot(a, b, preferred_element_type=F32)


def _bdot(a, b):
    return _dot(a.astype(BF16), b.astype(BF16))


def _bdot_nt(a, b):
    return lax.dot_general(a.astype(BF16), b.astype(BF16), (((1,), (1,)), ((), ())),
                           preferred_element_type=F32)


def _hdot(a, b):
    return jnp.dot(a, b, preferred_element_type=F32, precision=HIGHEST)


def _split(x, pieces):
    out = []
    for _ in range(pieces - 1):
        p = x.astype(BF16)
        out.append(p)
        x = x - p.astype(F32)
    out.append(x.astype(BF16))
    return out


def _dot_data_const(x, m, pieces):
    acc = None
    for p in _split(x, pieces):
        t = _dot(p, m)
        acc = t if acc is None else acc + t
    return acc


def _dot_const_data(m, x, pieces):
    acc = None
    for p in _split(x, pieces):
        t = _dot(m, p)
        acc = t if acc is None else acc + t
    return acc


def _sigmoid(x):
    return 1.0 / (1.0 + jnp.exp(-x))


def _silu(x):
    return x * _sigmoid(x)


def _softplus(x):
    return jnp.maximum(x, 0.0) + jnp.log(1.0 + jnp.exp(-jnp.abs(x)))


def _rms(x, gain):
    return x * lax.rsqrt(jnp.mean(x * x, axis=-1, keepdims=True) + EPS) * gain


def _params(sem):
    return pltpu.CompilerParams(dimension_semantics=sem, vmem_limit_bytes=VMEM_LIMIT)


def _const_spec(shape):
    nd = len(shape)
    return pl.BlockSpec(shape, lambda *_: (0,) * nd, pipeline_mode=pl.Buffered(1))


def _in_proj_kernel(x_ref, gain_ref, wg_ref, wa_ref, ws_ref, og_ref, oa_ref, os_ref):
    h = _rms(x_ref[...], gain_ref[...]).astype(BF16)
    og_ref[...] = _dot(h, wg_ref[...])
    oa_ref[...] = _dot(h, wa_ref[...]).astype(BF16)
    os_ref[...] = _dot(h, ws_ref[...])


def _in_proj(x, gain, wg, wa, ws):
    n = x.shape[0]
    tm = TOKEN_TILE
    row = lambda w: pl.BlockSpec((tm, w), lambda i: (i, 0))
    return pl.pallas_call(
        _in_proj_kernel,
        grid=(n // tm,),
        in_specs=[row(D_MODEL), _const_spec((1, D_MODEL)), _const_spec(wg.shape),
                  _const_spec(wa.shape), _const_spec(ws.shape)],
        out_specs=[row(GDN_W), row(ATT_W), row(SSD_W)],
        out_shape=[jax.ShapeDtypeStruct((n, GDN_W), F32), jax.ShapeDtypeStruct((n, ATT_W), BF16),
                   jax.ShapeDtypeStruct((n, SSD_W), F32)],
        compiler_params=_params(("parallel",)),
        name="in_proj",
    )(x, gain, wg, wa, ws)


def _conv_silu_into(src_ref, w_ref, bias, dst_ref, seq, chunk):
    w = w_ref[...]
    taps = [w[k:k + 1, :] for k in range(CONV_WIDTH)]

    def finish(acc):
        if bias is not None:
            acc = acc + bias
        return _silu(acc)

    blk = src_ref[0:chunk, :]
    rows = lax.broadcasted_iota(jnp.int32, (chunk, 1), 0)
    acc = taps[CONV_WIDTH - 1] * blk
    for k in range(CONV_WIDTH - 1):
        shift = CONV_WIDTH - 1 - k
        acc = acc + taps[k] * jnp.where(rows >= shift, pltpu.roll(blk, shift, 0), 0.0)
    dst_ref[0:chunk, :] = finish(acc)

    def body(c, carry):
        r0 = pl.multiple_of(c * chunk, chunk)
        blk = src_ref[pl.ds(r0 - CONV_HALO, chunk + CONV_HALO), :]
        acc = taps[CONV_WIDTH - 1] * blk[CONV_HALO:]
        for k in range(CONV_WIDTH - 1):
            acc = acc + taps[k] * pltpu.roll(blk, CONV_WIDTH - 1 - k, 0)[CONV_HALO:]
        dst_ref[pl.ds(r0, chunk), :] = finish(acc)
        return carry

    lax.fori_loop(1, seq // chunk, body, 0)


def _block_diag(x, mask):
    return jnp.where(mask, jnp.concatenate([x] * N_HEADS, axis=0), jnp.zeros((), x.dtype))


def _dot1_bd(a, b, mask):
    return _dot(a.astype(BF16), _block_diag(b.astype(BF16), mask))


def _dot3_bd(a, b, mask):
    ah, al = _split(a, 2)
    bh, bl = _split(b, 2)
    bdh = _block_diag(bh, mask)
    return _dot(ah, bdh) + _dot(al, bdh) + _dot(ah, _block_diag(bl, mask))


def _gdn_kernel(qkv_ref, z_ref, bg_ref, convw_ref, alog_ref, dtb_ref, norm_ref, o_ref,
                conv_ref, u_ref, wq_ref, aqk_ref, kdt_ref, gl_ref, s_ref, *, seq):
    C = GDN_CHUNK
    W = BRANCH_WIDTH
    NC = seq // C
    _conv_silu_into(qkv_ref, convw_ref, None, conv_ref, seq, 4 * C)

    def iota(shape, dim):
        return lax.broadcasted_iota(jnp.int32, shape, dim)

    same_head = (iota((W, W), 0) // HEAD_DIM) == (iota((W, W), 1) // HEAD_DIM)
    same_head2 = jnp.concatenate([same_head, same_head], axis=1)
    seg_ones = jnp.where(same_head, 1.0, 0.0).astype(BF16)
    r_c = iota((C, W), 0)
    j_c = iota((C, W), 1) % HEAD_DIM
    incl = r_c >= j_c
    strict = r_c > j_c
    eye = jnp.where(r_c == j_c, 1.0, 0.0)
    blocks = [(r_c // n) == (j_c // n) for n in (SUBLANES, 2 * SUBLANES, 4 * SUBLANES, C)]
    tril = jnp.where(iota((C, C), 0) >= iota((C, C), 1), 1.0, 0.0).astype(BF16)
    ones_c = jnp.ones((C, C), BF16)
    lane = iota((1, LANES), 1)
    ex_r = iota((LANES, 2 * W), 0)
    ex_c = iota((LANES, 2 * W), 1)
    expand = jnp.where(ex_r == jnp.where(ex_c < W, ex_c // HEAD_DIM, N_HEADS + (ex_c - W) // HEAD_DIM),
                       1.0, 0.0).astype(BF16)
    neg_a = -jnp.exp(alog_ref[...])
    dtb = dtb_ref[...]
    gain = norm_ref[...]
    scale = HEAD_DIM ** -0.5

    def prepare(i, carry):
        cs = [i * GDN_UNROLL + n for n in range(GDN_UNROLL)]
        r0s = [pl.multiple_of(c * C, C) for c in cs]
        each = lambda f, *xs: [f(*a) for a in zip(*xs)]
        qkv = [conv_ref[pl.ds(r0, C), :] for r0 in r0s]
        q = [a[:, :W] for a in qkv]
        k = [a[:, W:2 * W] for a in qkv]
        v = [a[:, 2 * W:] for a in qkv]
        ss = each(lambda q_, k_: _dot_data_const(jnp.concatenate([q_ * q_, k_ * k_], axis=0), seg_ones, 2), q, k)
        qn = each(lambda q_, s_: q_ * lax.rsqrt(s_[:C] + EPS) * scale, q, ss)
        kn = each(lambda k_, s_: k_ * lax.rsqrt(s_[C:] + EPS), k, ss)
        bg = [bg_ref[pl.ds(r0, C), :] for r0 in r0s]
        gc = each(lambda b_: _dot_const_data(tril, neg_a * _softplus(b_ + dtb), 3), bg)
        ex = each(lambda b_, g_: _dot_data_const(jnp.where(lane < N_HEADS, _sigmoid(b_), g_), expand, 3), bg, gc)
        beta = [a[:, :W] for a in ex]
        gcc = [a[:, W:] for a in ex]
        gcr = each(lambda g_: _dot_const_data(ones_c, g_ * eye, 3), gcc)
        dm = each(lambda a_, b_: jnp.exp(jnp.where(incl, a_ - b_, -jnp.inf)), gcc, gcr)
        kb = each(lambda a_, b_: a_ * b_, kn, beta)
        vb = each(lambda a_, b_: a_ * b_, v, beta)
        qk = each(lambda q_, kb_, kn_: lax.dot_general(
            jnp.concatenate([q_, kb_], axis=0).astype(BF16), _block_diag(kn_.astype(BF16), same_head),
            (((1,), (1,)), ((), ())), preferred_element_type=F32), qn, kb, kn)
        a_qk = each(lambda x_, d_: x_[:C] * d_, qk, dm)
        a = each(lambda x_, d_: jnp.where(strict, x_[C:] * d_, 0.0), qk, dm)
        d = [jnp.where(blocks[0], a_, 0.0) for a_ in a]
        p = [eye - d_ for d_ in d]
        b = each(lambda d_: _dot3_bd(d_, d_, same_head), d)
        pb = each(lambda p_, b_: _dot3_bd(jnp.concatenate([p_, b_], axis=0), b_, same_head), p, b)
        p = each(lambda p_, x_: p_ + x_[:C], p, pb)
        p = each(lambda p_, x_: p_ + _dot3_bd(p_, x_[C:], same_head), p, pb)
        for inner, outer in zip(blocks[:-1], blocks[1:]):
            pe = each(lambda p_, a_: _dot1_bd(p_, jnp.where(outer & ~inner, a_, 0.0), same_head), p, a)
            p = each(lambda p_, x_: p_ - _dot1_bd(x_, p_, same_head), p, pe)
        t_inv = p
        e_gc = [jnp.exp(g_) for g_ in gcc]
        uw = each(lambda t_, vb_, kb_, e_: _dot3_bd(t_, jnp.concatenate([vb_, kb_ * e_], axis=1), same_head2),
                  t_inv, vb, kb, e_gc)
        for n, c in enumerate(cs):
            r0 = r0s[n]
            g_last = gcc[n][C - 1:C, :]
            u_ref[pl.ds(r0, C), :] = uw[n][:, :W]
            wq_ref[pl.ds(pl.multiple_of(c * 2 * C, 2 * C), 2 * C), :] = jnp.concatenate(
                [uw[n][:, W:], qn[n] * e_gc[n]], axis=0).astype(BF16)
            aqk_ref[pl.ds(r0, C), :] = a_qk[n].astype(BF16)
            k_dec = jnp.concatenate([kn[n] * jnp.exp(g_last - gcc[n]), jnp.zeros((LANES - C, W), F32)], axis=0)
            kdt_ref[pl.ds(pl.multiple_of(c * W, W), W), :] = jnp.concatenate(
                [k_dec[:, :LANES].T, k_dec[:, LANES:].T], axis=0).astype(BF16)
            gl_ref[c] = jnp.broadcast_to(jnp.exp(g_last), (SUBLANES, W))
        return carry

    lax.fori_loop(0, NC // GDN_UNROLL, prepare, 0)

    s_ref[...] = jnp.zeros_like(s_ref)

    def recur(c, carry):
        r0 = pl.multiple_of(c * C, C)
        s = s_ref[...]
        ws_qs = _dot(wq_ref[pl.ds(pl.multiple_of(c * 2 * C, 2 * C), 2 * C), :], s.astype(BF16))
        v_new = (u_ref[pl.ds(r0, C), :] - ws_qs[:C]).astype(BF16)
        upd = _dot(kdt_ref[pl.ds(pl.multiple_of(c * W, W), W), :],
                   jnp.concatenate([v_new, jnp.zeros((LANES - C, W), BF16)], axis=0))
        s_ref[...] = s * gl_ref[c][0:1, :] + jnp.where(same_head, upd, 0.0)
        o = ws_qs[C:] + _dot(aqk_ref[pl.ds(r0, C), :], _block_diag(v_new, same_head))
        ms = _dot_data_const(o * o, seg_ones, 2) * (1.0 / HEAD_DIM)
        y = o * lax.rsqrt(ms + EPS) * gain * _silu(z_ref[pl.ds(r0, C), :])
        o_ref[pl.ds(r0, C), :] = y.astype(o_ref.dtype)
        return carry

    lax.fori_loop(0, NC, recur, 0)


def _gdn(pg, conv_w, a_row, dtb_row, norm_row, batch, seq):
    kern = functools.partial(_gdn_kernel, seq=seq)
    C = GDN_CHUNK
    nc = seq // C
    return pl.pallas_call(
        kern,
        grid=(batch,),
        in_specs=[pl.BlockSpec((seq, 3 * BRANCH_WIDTH), lambda b: (b, 0)),
                  pl.BlockSpec((seq, BRANCH_WIDTH), lambda b: (b, 3)),
                  pl.BlockSpec((seq, SMALL_PAD), lambda b: (b, 4 * BRANCH_WIDTH // SMALL_PAD)),
                  _const_spec(conv_w.shape), _const_spec(a_row.shape), _const_spec(dtb_row.shape),
                  _const_spec(norm_row.shape)],
        out_specs=pl.BlockSpec((seq, BRANCH_WIDTH), lambda b: (b, 0)),
        out_shape=jax.ShapeDtypeStruct((batch * seq, BRANCH_WIDTH), BF16),
        scratch_shapes=[pltpu.VMEM((seq, 3 * BRANCH_WIDTH), F32),
                        pltpu.VMEM((seq, BRANCH_WIDTH), F32),
                        pltpu.VMEM((2 * seq, BRANCH_WIDTH), BF16),
                        pltpu.VMEM((seq, BRANCH_WIDTH), BF16),
                        pltpu.VMEM((nc * BRANCH_WIDTH, LANES), BF16),
                        pltpu.VMEM((nc, SUBLANES, BRANCH_WIDTH), F32),
                        pltpu.VMEM((BRANCH_WIDTH, BRANCH_WIDTH), F32)],
        compiler_params=_params(("parallel",)),
        name="gdn",
    )(pg, pg, pg, conv_w, a_row, dtb_row, norm_row)


def _ssd_kernel(xbc_ref, z_ref, dt_ref, convw_ref, convb_ref, alog_ref, dtb_ref, dskip_ref, norm_ref,
                o_ref, conv_ref, s_ref, *, seq):
    L = SSD_CHUNK
    W = BRANCH_WIDTH
    GN = SSD_GROUPS * SSD_STATE
    _conv_silu_into(xbc_ref, convw_ref, convb_ref[...], conv_ref, seq, L)
    s_ref[...] = jnp.zeros_like(s_ref)

    ri = lax.broadcasted_iota(jnp.int32, (L, L), 0)
    ci = lax.broadcasted_iota(jnp.int32, (L, L), 1)
    incl = ri >= ci
    tril = jnp.where(incl, 1.0, 0.0)
    first = lax.broadcasted_iota(jnp.int32, (1, LANES), 1) < HEAD_DIM
    neg_a = -jnp.exp(alog_ref[...])
    dtb = dtb_ref[...]
    dskip = dskip_ref[...]
    gain = norm_ref[...]

    def body(c, carry):
        r0 = pl.multiple_of(c * L, L)
        xbc = conv_ref[pl.ds(r0, L), :]
        zt = z_ref[pl.ds(r0, L), :]
        dt = _softplus(dt_ref[pl.ds(r0, L), :] + dtb)
        acs = _hdot(tril, dt * neg_a)
        acs_rows = acs.T
        a_last = acs[L - 1:L, :]
        e_acs = jnp.exp(acs)
        dt_rem = dt * jnp.exp(a_last - acs)
        e_last = jnp.exp(a_last)
        ys = []
        for g in range(SSD_GROUPS):
            h0, h1 = PAIR * g, PAIR * g + 1
            x2 = xbc[:, g * LANES:(g + 1) * LANES]
            bm = xbc[:, W + g * SSD_STATE:W + (g + 1) * SSD_STATE]
            cm = xbc[:, W + GN + g * SSD_STATE:W + GN + (g + 1) * SSD_STATE]
            cb = _bdot_nt(cm, bm)
            xdt = x2 * jnp.where(first, dt[:, h0:h0 + 1], dt[:, h1:h1 + 1])
            yd = []
            for h in (h0, h1):
                lmat = jnp.exp(jnp.where(incl, acs[:, h:h + 1] - acs_rows[h:h + 1, :], -jnp.inf))
                yd.append(_bdot(cb * lmat, xdt))
            y_diag = jnp.where(first, yd[0], yd[1])
            s = s_ref[g]
            y_off = _bdot(cm, s) * jnp.where(first, e_acs[:, h0:h0 + 1], e_acs[:, h1:h1 + 1])
            x_rem = x2 * jnp.where(first, dt_rem[:, h0:h0 + 1], dt_rem[:, h1:h1 + 1])
            s_ref[g] = (s * jnp.where(first, e_last[:, h0:h0 + 1], e_last[:, h1:h1 + 1])
                        + _bdot(bm.T, x_rem))
            ys.append(y_diag + y_off)
        y = jnp.concatenate(ys, axis=1) + dskip * xbc[:, :W]
        o_ref[pl.ds(r0, L), :] = _rms(y * _silu(zt), gain).astype(o_ref.dtype)
        return carry

    lax.fori_loop(0, seq // L, body, 0)


def _ssd(ps, conv_w, conv_b, a_row, dtb_row, dskip_row, norm_row, batch, seq):
    kern = functools.partial(_ssd_kernel, seq=seq)
    return pl.pallas_call(
        kern,
        grid=(batch,),
        in_specs=[pl.BlockSpec((seq, SSD_CONV_DIM), lambda b: (b, 0)),
                  pl.BlockSpec((seq, BRANCH_WIDTH), lambda b: (b, SSD_CONV_DIM // BRANCH_WIDTH)),
                  pl.BlockSpec((seq, SMALL_PAD), lambda b: (b, (SSD_CONV_DIM + BRANCH_WIDTH) // SMALL_PAD)),
                  _const_spec(conv_w.shape), _const_spec(conv_b.shape), _const_spec(a_row.shape),
                  _const_spec(dtb_row.shape), _const_spec(dskip_row.shape), _const_spec(norm_row.shape)],
        out_specs=pl.BlockSpec((seq, BRANCH_WIDTH), lambda b: (b, 0)),
        out_shape=jax.ShapeDtypeStruct((batch * seq, BRANCH_WIDTH), BF16),
        scratch_shapes=[pltpu.VMEM((seq, SSD_CONV_DIM), F32),
                        pltpu.VMEM((SSD_GROUPS, SSD_STATE, LANES), F32)],
        compiler_params=_params(("parallel",)),
        name="ssd",
    )(ps, ps, ps, conv_w, conv_b, a_row, dtb_row, dskip_row, norm_row)


def _head_masked(q, first_half):
    lane = lax.broadcasted_iota(jnp.int32, (1, LANES), 1)
    keep = (lane < HEAD_DIM) if first_half else (lane >= HEAD_DIM)
    return jnp.where(keep, q.astype(F32), 0.0)


def _moba_kernel(q_ref, k_ref, v_ref, o_ref, kmean_ref, *, seq):
    BS = MOBA_BLOCK
    NB = seq // BS
    W = BRANCH_WIDTH
    qb = pl.program_id(1)
    scale = HEAD_DIM ** -0.5
    q = q_ref[...]

    @pl.when(qb == 0)
    def _():
        kmean = jnp.sum(k_ref[...].astype(F32).reshape(NB, BS, W), axis=1) * (1.0 / BS)
        kmean_ref[...] = jnp.concatenate([kmean, jnp.zeros((LANES - NB, W), F32)], axis=0)

    lane = lax.broadcasted_iota(jnp.int32, (1, LANES), 1)
    rel = (lax.broadcasted_iota(jnp.int32, (BS, BS), 0)
           - lax.broadcasted_iota(jnp.int32, (BS, BS), 1)).astype(F32)
    blk_row = lax.broadcasted_iota(jnp.int32, (LANES, BS), 0)
    blk_id = lax.broadcasted_iota(jnp.int32, (SUBLANES, BS), 0)
    heads = range(N_HEADS)
    pair_of = [h // PAIR for h in heads]
    lanes_of = lambda x, p: x[:, p * LANES:(p + 1) * LANES]
    slopes = [2.0 ** (-8.0 * (h + 1) / N_HEADS) for h in heads]

    qf = [_head_masked(lanes_of(q, pair_of[h]), h % PAIR == 0) for h in heads]
    gate = [lax.dot_general(lanes_of(kmean_ref[...], pair_of[h]), qf[h], (((1,), (1,)), ((), ())),
                            preferred_element_type=F32, precision=HIGHEST)[:NB] for h in heads]
    sels = []
    for h in heads:
        rank = jnp.zeros((NB, BS), F32)
        for i in range(NB):
            gi = gate[h][i:i + 1, :]
            beats = jnp.where(gi > gate[h], 1.0, jnp.where((gi == gate[h]) & (blk_id > i), 1.0, 0.0))
            rank = rank + jnp.where(i < qb, beats, 0.0)
        sel_t = jnp.where((rank < MOBA_TOPK) & (blk_id < qb), 1.0, 0.0)
        sel_t = jnp.concatenate([sel_t, jnp.zeros((LANES - NB, BS), F32)], axis=0)
        sels.append(jnp.concatenate([sel_t[:, :LANES].T, sel_t[:, LANES:].T], axis=0).astype(BF16))
    qms = [(qf[h] * scale).astype(BF16) for h in heads]

    own0 = pl.multiple_of(qb * BS, BS)
    k_own = k_ref[pl.ds(own0, BS), :]
    v_own = v_ref[pl.ds(own0, BS), :]
    s = [_bdot_nt(qms[h], lanes_of(k_own, pair_of[h])) for h in heads]
    s = [jnp.where(rel >= 0, s[h] - slopes[h] * rel, -jnp.inf) for h in heads]
    m = [jnp.max(s_, axis=-1, keepdims=True) for s_ in s]
    p = [jnp.exp(s[h] - m[h]) for h in heads]
    l = [jnp.sum(p_, axis=-1, keepdims=True) for p_ in p]
    acc = [_bdot(p[h], lanes_of(v_own, pair_of[h])) for h in heads]

    def body(j, carry):
        r0 = pl.multiple_of(j * BS, BS)
        k_j = k_ref[pl.ds(r0, BS), :]
        v_j = v_ref[pl.ds(r0, BS), :]
        onehot = jnp.where(blk_row == j, 1.0, 0.0).astype(BF16)
        dist = rel + ((qb - j) * BS).astype(F32)
        m, l, acc = carry[:N_HEADS], carry[N_HEADS:2 * N_HEADS], carry[2 * N_HEADS:]

        def scores(h):
            return _bdot_nt(qms[h], lanes_of(k_j, pair_of[h])), _dot(sels[h], onehot)

        def finish(h, raw):
            s = jnp.where(raw[1] > 0.5, raw[0] - slopes[h] * dist, -jnp.inf)
            m_new = jnp.maximum(m[h], jnp.max(s, axis=-1, keepdims=True))
            alpha = jnp.exp(m[h] - m_new)
            p = jnp.exp(s - m_new)
            l_new = l[h] * alpha + jnp.sum(p, axis=-1, keepdims=True)
            return m_new, l_new, acc[h] * alpha + _bdot(p, lanes_of(v_j, pair_of[h]))

        raw = scores(0)
        done = []
        for h in heads:
            nxt = scores(h + 1) if h + 1 < N_HEADS else None
            done.append(finish(h, raw))
            raw = nxt
        return tuple(d[0] for d in done) + tuple(d[1] for d in done) + tuple(d[2] for d in done)

    out = lax.fori_loop(0, qb, body, tuple(m) + tuple(l) + tuple(acc))
    l, acc = out[N_HEADS:2 * N_HEADS], out[2 * N_HEADS:]
    o = [acc[h] / l[h] for h in heads]
    o_ref[...] = jnp.concatenate(
        [jnp.where(lane < HEAD_DIM, o[PAIR * p_], o[PAIR * p_ + 1]) for p_ in range(N_HEADS // PAIR)],
        axis=1).astype(o_ref.dtype)


def _moba(pa, batch, seq):
    BS = MOBA_BLOCK
    nq = seq // BS
    assert nq == SUBLANES
    W = BRANCH_WIDTH
    kern = functools.partial(_moba_kernel, seq=seq)
    return pl.pallas_call(
        kern,
        grid=(batch, nq),
        in_specs=[pl.BlockSpec((BS, W), lambda b, i: (b * nq + i, 0)),
                  pl.BlockSpec((seq, W), lambda b, i: (b, 1)),
                  pl.BlockSpec((seq, W), lambda b, i: (b, 2))],
        out_specs=pl.BlockSpec((BS, W), lambda b, i: (b * nq + i, 0)),
        out_shape=jax.ShapeDtypeStruct((batch * seq, W), BF16),
        scratch_shapes=[pltpu.VMEM((LANES, W), F32)],
        compiler_params=_params(("parallel", "arbitrary")),
        name="moba",
    )(pa, pa, pa)


def _sb_kernel(q_ref, k_ref, v_ref, o_ref):
    TQ = SB_TILE
    qb = pl.program_id(1)
    scale = HEAD_DIM ** -0.5
    q = q_ref[...]
    below = lax.broadcasted_iota(jnp.int32, (TQ, TQ), 0) > lax.broadcasted_iota(jnp.int32, (TQ, TQ), 1)
    later = jnp.where(below, 1.0, 0.0).astype(BF16)
    lane = lax.broadcasted_iota(jnp.int32, (1, LANES), 1)
    heads = range(N_HEADS)
    pair_of = [h // PAIR for h in heads]
    lanes_of = lambda x, p: x[:, p * LANES:(p + 1) * LANES]
    qms = [(_head_masked(lanes_of(q, pair_of[h]), h % PAIR == 0) * scale).astype(BF16) for h in heads]

    def tile(kb, carry, diagonal):
        r0 = pl.multiple_of(kb * TQ, TQ)
        k_j = k_ref[pl.ds(r0, TQ), :]
        v_j = v_ref[pl.ds(r0, TQ), :]
        tails, accs = carry[:N_HEADS], carry[N_HEADS:]
        z2 = [_bdot_nt(qms[h], lanes_of(k_j, pair_of[h])) * LOG2E for h in heads]
        cost = [jnp.maximum(z_, 0.0) + jnp.log2(1.0 + jnp.exp2(-jnp.abs(z_))) for z_ in z2]
        if diagonal:
            cost = [jnp.where(below, c_, 0.0) for c_ in cost]
        after = [_dot_data_const(c_, later, 2) for c_ in cost]
        w = [jnp.exp2(z2[h] - cost[h] - after[h] - tails[h]) for h in heads]
        if diagonal:
            w = [jnp.where(below, w_, 0.0) for w_ in w]
        accs = [accs[h] + _bdot(w[h], lanes_of(v_j, pair_of[h])) for h in heads]
        tails = [tails[h] + jnp.sum(cost[h], axis=-1, keepdims=True) for h in heads]
        return tuple(tails) + tuple(accs)

    def live(tails):
        least = tails[0]
        for t_ in tails[1:]:
            least = jnp.minimum(least, t_)
        return jnp.min(least) < SB_DEAD_TAIL

    init = tuple(jnp.zeros((TQ, 1), F32) for _ in heads) + tuple(jnp.zeros((TQ, LANES), F32) for _ in heads)
    carry = tile(qb, init, True)

    def step(state):
        i = state[0]
        new = tile(qb - 1 - i, state[2:], False)
        return (i + 1, live(new[:N_HEADS])) + new

    out = lax.while_loop(lambda state: jnp.logical_and(state[0] < qb, state[1]), step,
                         (jnp.int32(0), live(carry[:N_HEADS])) + carry)
    accs = out[2 + N_HEADS:]
    o_ref[...] = jnp.concatenate(
        [jnp.where(lane < HEAD_DIM, accs[PAIR * p], accs[PAIR * p + 1]) for p in range(N_HEADS // PAIR)],
        axis=1).astype(o_ref.dtype)


def _sb(pa, batch, seq):
    TQ = SB_TILE
    nq = seq // TQ
    W = BRANCH_WIDTH
    return pl.pallas_call(
        _sb_kernel,
        grid=(batch, nq),
        in_specs=[pl.BlockSpec((TQ, W), lambda b, i: (b * nq + i, 3)),
                  pl.BlockSpec((seq, W), lambda b, i: (b, 4)),
                  pl.BlockSpec((seq, W), lambda b, i: (b, 5))],
        out_specs=pl.BlockSpec((TQ, W), lambda b, i: (b * nq + i, 0)),
        out_shape=jax.ShapeDtypeStruct((batch * seq, W), BF16),
        compiler_params=_params(("parallel", "arbitrary")),
        name="sb",
    )(pa, pa, pa)


def _merge_kernel(x_ref, ya_ref, yb_ref, yc_ref, yd_ref, gpre_ref, wgate_ref, wbr_ref, wout_ref, gpost_ref, o_ref):
    x = x_ref[...]
    h = _rms(x, gpre_ref[...]).astype(BF16)
    merged = None
    for g, y_ref in enumerate((ya_ref, yb_ref, yc_ref, yd_ref)):
        gate = _sigmoid(_dot(h, wgate_ref[g]))
        term = gate * _dot(y_ref[...], wbr_ref[g])
        merged = term if merged is None else merged + term
    mix = _dot(merged.astype(BF16), wout_ref[...])
    o_ref[...] = x + _rms(mix, gpost_ref[...])


def _merge(x, ya, yb, yc, yd, gpre, wgate, wbr, wout, gpost):
    n = x.shape[0]
    tm = TOKEN_TILE
    row = lambda w: pl.BlockSpec((tm, w), lambda i: (i, 0))
    return pl.pallas_call(
        _merge_kernel,
        grid=(n // tm,),
        in_specs=[row(D_MODEL)] + [row(BRANCH_WIDTH)] * N_BRANCH
        + [_const_spec(gpre.shape), _const_spec(wgate.shape), _const_spec(wbr.shape),
           _const_spec(wout.shape), _const_spec(gpost.shape)],
        out_specs=row(D_MODEL),
        out_shape=jax.ShapeDtypeStruct((n, D_MODEL), F32),
        compiler_params=_params(("parallel",)),
        name="merge",
    )(x, ya, yb, yc, yd, gpre, wgate, wbr, wout, gpost)


def _ffn_kernel(x_ref, gpre_ref, wup_ref, wdown_ref, gpost_ref, o_ref):
    x = x_ref[...]
    h = _rms(x, gpre_ref[...]).astype(BF16)
    f = None
    for c in range(D_FF // FF_CHUNK):
        u = _dot(h, wup_ref[:, c * FF_CHUNK:(c + 1) * FF_CHUNK])
        a = jnp.square(jnp.maximum(u, 0.0)).astype(BF16)
        t = _dot(a, wdown_ref[c * FF_CHUNK:(c + 1) * FF_CHUNK, :])
        f = t if f is None else f + t
    o_ref[...] = x + _rms(f, gpost_ref[...])


def _ffn(x, gpre, wup, wdown, gpost):
    n = x.shape[0]
    tm = TOKEN_TILE
    row = pl.BlockSpec((tm, D_MODEL), lambda i: (i, 0))
    return pl.pallas_call(
        _ffn_kernel,
        grid=(n // tm,),
        in_specs=[row, _const_spec(gpre.shape), _const_spec(wup.shape), _const_spec(wdown.shape),
                  _const_spec(gpost.shape)],
        out_specs=row,
        out_shape=jax.ShapeDtypeStruct((n, D_MODEL), F32),
        compiler_params=_params(("parallel",)),
        name="ffn",
    )(x, gpre, wup, wdown, gpost)


def _pad_lanes(a, width):
    return jnp.pad(a, ((0, 0), (0, width - a.shape[1])))


def _layer_weights(w_in_l):
    W = BRANCH_WIDTH
    gdn_in = 4 * W + 2 * N_HEADS
    off_moba = gdn_in
    off_sb = off_moba + 3 * W
    off_ssd = off_sb + 3 * W
    wg = jnp.concatenate([w_in_l[:, :4 * W], _pad_lanes(w_in_l[:, 4 * W:gdn_in], SMALL_PAD)], axis=1)
    wa = w_in_l[:, off_moba:off_ssd]
    ssd = w_in_l[:, off_ssd:]
    ws = jnp.concatenate([ssd[:, W:W + SSD_CONV_DIM], ssd[:, :W],
                          _pad_lanes(ssd[:, W + SSD_CONV_DIM:], SMALL_PAD)], axis=1)
    return wg.astype(BF16), wa.astype(BF16), ws.astype(BF16)


def kernel(x, norm_mix_pre, norm_mix_post, norm_ffn_pre, norm_ffn_post, w_in, gdn_conv, gdn_a_log, gdn_dt_bias,
           gdn_norm, ssd_conv, ssd_conv_bias, ssd_a_log, ssd_dt_bias, ssd_d, ssd_norm, w_gate, w_branch, w_out,
           w_up, w_down):
    batch, seq, d = x.shape
    assert d == D_MODEL and seq % MOBA_BLOCK == 0 and (batch * seq) % TOKEN_TILE == 0
    depth = w_in.shape[0]
    xf = x.reshape(batch * seq, d).astype(F32)
    row = lambda a: a.reshape(1, -1).astype(F32)
    for l in range(depth):
        wg, wa, ws = _layer_weights(w_in[l])
        pg, pa, ps = _in_proj(xf, row(norm_mix_pre[l]), wg, wa, ws)
        gdn_a = _pad_lanes(jnp.concatenate([jnp.zeros((N_HEADS,), F32), gdn_a_log[l]]).reshape(1, -1), SMALL_PAD)
        gdn_b = _pad_lanes(jnp.concatenate([jnp.zeros((N_HEADS,), F32), gdn_dt_bias[l]]).reshape(1, -1), SMALL_PAD)
        ya = _gdn(pg, gdn_conv[l].astype(F32), gdn_a, gdn_b, row(jnp.tile(gdn_norm[l], N_HEADS)), batch, seq)
        yb = _moba(pa, batch, seq)
        yc = _sb(pa, batch, seq)
        yd = _ssd(ps, ssd_conv[l].astype(F32), row(ssd_conv_bias[l]),
                  _pad_lanes(row(ssd_a_log[l]), SMALL_PAD), _pad_lanes(row(ssd_dt_bias[l]), SMALL_PAD),
                  row(jnp.repeat(ssd_d[l], HEAD_DIM)), row(ssd_norm[l]), batch, seq)
        xf = _merge(xf, ya, yb, yc, yd, row(norm_mix_pre[l]), w_gate[l].astype(BF16), w_branch[l].astype(BF16),
                    w_out[l].astype(BF16), row(norm_mix_post[l]))
        xf = _ffn(xf, row(norm_ffn_pre[l]), w_up[l].astype(BF16), w_down[l].astype(BF16), row(norm_ffn_post[l]))
    return xf.reshape(batch, seq, d).astype(x.dtype)
```

```python
import functools

import jax
import jax.numpy as jnp
from jax import lax
from jax.experimental import pallas as pl
from jax.experimental.pallas import tpu as pltpu

F32 = jnp.float32
BF16 = jnp.bfloat16
HIGHEST = lax.Precision.HIGHEST

D_MODEL = 1024
N_BRANCH = 4
BRANCH_WIDTH = 256
HEAD_DIM = 64
N_HEADS = 4
CONV_WIDTH = 4
GDN_CHUNK = 64
MOBA_BLOCK = 256
MOBA_TOPK = 3
SSD_STATE = 128
SSD_GROUPS = 2
SSD_CHUNK = 128
D_FF = 4 * D_MODEL
EPS = 1e-6

LANES = 128
SUBLANES = 8
PAIR = LANES // HEAD_DIM
SMALL_PAD = LANES
CONV_HALO = 8
TOKEN_TILE = 512
FF_CHUNK = 1024
SB_TILE = 256
LOG2E = 1.4426950408889634
SB_DEAD_TAIL = 256.0
GDN_UNROLL = 8
VMEM_LIMIT = 56 * 1024 * 1024

GDN_W = 3 * BRANCH_WIDTH + BRANCH_WIDTH + SMALL_PAD
ATT_W = 6 * BRANCH_WIDTH
SSD_CONV_DIM = BRANCH_WIDTH + 2 * SSD_GROUPS * SSD_STATE
SSD_W = SSD_CONV_DIM + BRANCH_WIDTH + SMALL_PAD


def _dot(a, b):
    return jnp.dot(a, b, preferred_element_type=F32)


def _bdot(a, b):
    return _dot(a.astype(BF16), b.astype(BF16))


def _bdot_nt(a, b):
    return lax.dot_general(a.astype(BF16), b.astype(BF16), (((1,), (1,)), ((), ())),
                           preferred_element_type=F32)


def _hdot(a, b):
    return jnp.dot(a, b, preferred_element_type=F32, precision=HIGHEST)


def _split(x, pieces):
    out = []
    for _ in range(pieces - 1):
        p = x.astype(BF16)
        out.append(p)
        x = x - p.astype(F32)
    out.append(x.astype(BF16))
    return out


def _dot_data_const(x, m, pieces):
    acc = None
    for p in _split(x, pieces):
        t = _dot(p, m)
        acc = t if acc is None else acc + t
    return acc


def _dot_const_data(m, x, pieces):
    acc = None
    for p in _split(x, pieces):
        t = _dot(m, p)
        acc = t if acc is None else acc + t
    return acc


def _sigmoid(x):
    return 1.0 / (1.0 + jnp.exp(-x))


def _silu(x):
    return x * _sigmoid(x)


def _softplus(x):
    return jnp.maximum(x, 0.0) + jnp.log(1.0 + jnp.exp(-jnp.abs(x)))


def _rms(x, gain):
    return x * lax.rsqrt(jnp.mean(x * x, axis=-1, keepdims=True) + EPS) * gain


def _params(sem):
    return pltpu.CompilerParams(dimension_semantics=sem, vmem_limit_bytes=VMEM_LIMIT)


def _const_spec(shape):
    nd = len(shape)
    return pl.BlockSpec(shape, lambda *_: (0,) * nd, pipeline_mode=pl.Buffered(1))


def _cols(w_ref, start, width):
    base = start // LANES * LANES
    stop = min(-(-(start + width) // LANES) * LANES, w_ref.shape[1])
    return w_ref[:, base:stop][:, start - base:start - base + width].astype(BF16)


def _in_proj_kernel(x_ref, gain_ref, w_ref, wgs_ref, wss_ref, og_ref, oa_ref, os_ref, wg_ref, wa_ref, ws_ref):
    W = BRANCH_WIDTH
    main = 4 * W

    @pl.when(pl.program_id(0) == 0)
    def _():
        off_att = main + 2 * N_HEADS
        off_ssd = off_att + ATT_W
        for c in range(main // W):
            wg_ref[:, c * W:(c + 1) * W] = _cols(w_ref, c * W, W)
        for c in range(ATT_W // W):
            wa_ref[:, c * W:(c + 1) * W] = _cols(w_ref, off_att + c * W, W)
        for c in range(SSD_CONV_DIM // W):
            ws_ref[:, c * W:(c + 1) * W] = _cols(w_ref, off_ssd + W + c * W, W)
        ws_ref[:, SSD_CONV_DIM:main] = _cols(w_ref, off_ssd, W)

    h = _rms(x_ref[...], gain_ref[...]).astype(BF16)
    og_ref[:, :main] = _dot(h, wg_ref[...])
    og_ref[:, main:] = _dot(h, wgs_ref[...])
    oa_ref[...] = _dot(h, wa_ref[...]).astype(BF16)
    os_ref[:, :main] = _dot(h, ws_ref[...])
    os_ref[:, main:] = _dot(h, wss_ref[...])


def _in_proj(x, gain, w_in, layer, wg_small, ws_small):
    n = x.shape[0]
    tm = TOKEN_TILE
    row = lambda w: pl.BlockSpec((tm, w), lambda i: (i, 0))
    main = 4 * BRANCH_WIDTH
    return pl.pallas_call(
        _in_proj_kernel,
        grid=(n // tm,),
        in_specs=[row(D_MODEL), _const_spec((1, D_MODEL)),
                  pl.BlockSpec((None,) + w_in.shape[1:], lambda i: (layer, 0, 0), pipeline_mode=pl.Buffered(1)),
                  _const_spec(wg_small.shape), _const_spec(ws_small.shape)],
        out_specs=[row(GDN_W), row(ATT_W), row(SSD_W)],
        out_shape=[jax.ShapeDtypeStruct((n, GDN_W), F32), jax.ShapeDtypeStruct((n, ATT_W), BF16),
                   jax.ShapeDtypeStruct((n, SSD_W), F32)],
        scratch_shapes=[pltpu.VMEM((D_MODEL, main), BF16), pltpu.VMEM((D_MODEL, ATT_W), BF16),
                        pltpu.VMEM((D_MODEL, main), BF16)],
        compiler_params=_params(("arbitrary",)),
        name="in_proj",
    )(x, gain, w_in, wg_small, ws_small)


def _conv_silu_into(src_ref, w_ref, bias, dst_ref, seq, chunk):
    w = w_ref[...]
    taps = [w[k:k + 1, :] for k in range(CONV_WIDTH)]

    def finish(acc):
        if bias is not None:
            acc = acc + bias
        return _silu(acc)

    blk = src_ref[0:chunk, :]
    rows = lax.broadcasted_iota(jnp.int32, (chunk, 1), 0)
    acc = taps[CONV_WIDTH - 1] * blk
    for k in range(CONV_WIDTH - 1):
        shift = CONV_WIDTH - 1 - k
        acc = acc + taps[k] * jnp.where(rows >= shift, pltpu.roll(blk, shift, 0), 0.0)
    dst_ref[0:chunk, :] = finish(acc)

    def body(c, carry):
        r0 = pl.multiple_of(c * chunk, chunk)
        blk = src_ref[pl.ds(r0 - CONV_HALO, chunk + CONV_HALO), :]
        acc = taps[CONV_WIDTH - 1] * blk[CONV_HALO:]
        for k in range(CONV_WIDTH - 1):
            acc = acc + taps[k] * pltpu.roll(blk, CONV_WIDTH - 1 - k, 0)[CONV_HALO:]
        dst_ref[pl.ds(r0, chunk), :] = finish(acc)
        return carry

    lax.fori_loop(1, seq // chunk, body, 0)


def _block_diag(x, mask):
    return jnp.where(mask, jnp.concatenate([x] * N_HEADS, axis=0), jnp.zeros((), x.dtype))


def _dot1_bd(a, b, mask):
    return _dot(a.astype(BF16), _block_diag(b.astype(BF16), mask))


def _dot3_bd(a, b, mask):
    ah, al = _split(a, 2)
    bh, bl = _split(b, 2)
    bdh = _block_diag(bh, mask)
    return _dot(ah, bdh) + _dot(al, bdh) + _dot(ah, _block_diag(bl, mask))


def _gdn_kernel(qkv_ref, z_ref, bg_ref, convw_ref, alog_ref, dtb_ref, norm_ref, o_ref,
                conv_ref, u_ref, wq_ref, aqk_ref, kdt_ref, gl_ref, s_ref, *, seq):
    C = GDN_CHUNK
    W = BRANCH_WIDTH
    NC = seq // C
    _conv_silu_into(qkv_ref, convw_ref, None, conv_ref, seq, 4 * C)

    def iota(shape, dim):
        return lax.broadcasted_iota(jnp.int32, shape, dim)

    same_head = (iota((W, W), 0) // HEAD_DIM) == (iota((W, W), 1) // HEAD_DIM)
    same_head2 = jnp.concatenate([same_head, same_head], axis=1)
    seg_ones = jnp.where(same_head, 1.0, 0.0).astype(BF16)
    r_c = iota((C, W), 0)
    j_c = iota((C, W), 1) % HEAD_DIM
    incl = r_c >= j_c
    strict = r_c > j_c
    eye = jnp.where(r_c == j_c, 1.0, 0.0)
    blocks = [(r_c // n) == (j_c // n) for n in (SUBLANES, 2 * SUBLANES, 4 * SUBLANES, C)]
    tril = jnp.where(iota((C, C), 0) >= iota((C, C), 1), 1.0, 0.0).astype(BF16)
    ones_c = jnp.ones((C, C), BF16)
    lane = iota((1, LANES), 1)
    ex_r = iota((LANES, 2 * W), 0)
    ex_c = iota((LANES, 2 * W), 1)
    expand = jnp.where(ex_r == jnp.where(ex_c < W, ex_c // HEAD_DIM, N_HEADS + (ex_c - W) // HEAD_DIM),
                       1.0, 0.0).astype(BF16)
    neg_a = -jnp.exp(alog_ref[...])
    dtb = dtb_ref[...]
    gain = norm_ref[...]
    scale = HEAD_DIM ** -0.5

    def prepare(i, carry):
        cs = [i * GDN_UNROLL + n for n in range(GDN_UNROLL)]
        r0s = [pl.multiple_of(c * C, C) for c in cs]
        each = lambda f, *xs: [f(*a) for a in zip(*xs)]
        qkv = [conv_ref[pl.ds(r0, C), :] for r0 in r0s]
        q = [a[:, :W] for a in qkv]
        k = [a[:, W:2 * W] for a in qkv]
        v = [a[:, 2 * W:] for a in qkv]
        ss = each(lambda q_, k_: _dot_data_const(jnp.concatenate([q_ * q_, k_ * k_], axis=0), seg_ones, 2), q, k)
        qn = each(lambda q_, s_: q_ * lax.rsqrt(s_[:C] + EPS) * scale, q, ss)
        kn = each(lambda k_, s_: k_ * lax.rsqrt(s_[C:] + EPS), k, ss)
        bg = [bg_ref[pl.ds(r0, C), :] for r0 in r0s]
        gc = each(lambda b_: _dot_const_data(tril, neg_a * _softplus(b_ + dtb), 3), bg)
        ex = each(lambda b_, g_: _dot_data_const(jnp.where(lane < N_HEADS, _sigmoid(b_), g_), expand, 3), bg, gc)
        beta = [a[:, :W] for a in ex]
        gcc = [a[:, W:] for a in ex]
        gcr = each(lambda g_: _dot_const_data(ones_c, g_ * eye, 3), gcc)
        dm = each(lambda a_, b_: jnp.exp(jnp.where(incl, a_ - b_, -jnp.inf)), gcc, gcr)
        kb = each(lambda a_, b_: a_ * b_, kn, beta)
        vb = each(lambda a_, b_: a_ * b_, v, beta)
        qk = each(lambda q_, kb_, kn_: lax.dot_general(
            jnp.concatenate([q_, kb_], axis=0).astype(BF16), _block_diag(kn_.astype(BF16), same_head),
            (((1,), (1,)), ((), ())), preferred_element_type=F32), qn, kb, kn)
        a_qk = each(lambda x_, d_: x_[:C] * d_, qk, dm)
        a = each(lambda x_, d_: jnp.where(strict, x_[C:] * d_, 0.0), qk, dm)
        d = [jnp.where(blocks[0], a_, 0.0) for a_ in a]
        p = [eye - d_ for d_ in d]
        b = each(lambda d_: _dot3_bd(d_, d_, same_head), d)
        pb = each(lambda p_, b_: _dot3_bd(jnp.concatenate([p_, b_], axis=0), b_, same_head), p, b)
        p = each(lambda p_, x_: p_ + x_[:C], p, pb)
        p = each(lambda p_, x_: p_ + _dot3_bd(p_, x_[C:], same_head), p, pb)
        for inner, outer in zip(blocks[:-1], blocks[1:]):
            pe = each(lambda p_, a_: _dot1_bd(p_, jnp.where(outer & ~inner, a_, 0.0), same_head), p, a)
            p = each(lambda p_, x_: p_ - _dot1_bd(x_, p_, same_head), p, pe)
        t_inv = p
        e_gc = [jnp.exp(g_) for g_ in gcc]
        uw = each(lambda t_, vb_, kb_, e_: _dot3_bd(t_, jnp.concatenate([vb_, kb_ * e_], axis=1), same_head2),
                  t_inv, vb, kb, e_gc)
        for n, c in enumerate(cs):
            r0 = r0s[n]
            g_last = gcc[n][C - 1:C, :]
            u_ref[pl.ds(r0, C), :] = uw[n][:, :W]
            wq_ref[pl.ds(pl.multiple_of(c * 2 * C, 2 * C), 2 * C), :] = jnp.concatenate(
                [uw[n][:, W:], qn[n] * e_gc[n]], axis=0).astype(BF16)
            aqk_ref[pl.ds(r0, C), :] = a_qk[n].astype(BF16)
            k_dec = jnp.concatenate([kn[n] * jnp.exp(g_last - gcc[n]), jnp.zeros((LANES - C, W), F32)], axis=0)
            kdt_ref[pl.ds(pl.multiple_of(c * W, W), W), :] = jnp.concatenate(
                [k_dec[:, :LANES].T, k_dec[:, LANES:].T], axis=0).astype(BF16)
            gl_ref[c] = jnp.broadcast_to(jnp.exp(g_last), (SUBLANES, W))
        return carry

    lax.fori_loop(0, NC // GDN_UNROLL, prepare, 0)

    s_ref[...] = jnp.zeros_like(s_ref)

    def recur(c, carry):
        r0 = pl.multiple_of(c * C, C)
        s = s_ref[...]
        ws_qs = _dot(wq_ref[pl.ds(pl.multiple_of(c * 2 * C, 2 * C), 2 * C), :], s.astype(BF16))
        v_new = (u_ref[pl.ds(r0, C), :] - ws_qs[:C]).astype(BF16)
        upd = _dot(kdt_ref[pl.ds(pl.multiple_of(c * W, W), W), :],
                   jnp.concatenate([v_new, jnp.zeros((LANES - C, W), BF16)], axis=0))
        s_ref[...] = s * gl_ref[c][0:1, :] + jnp.where(same_head, upd, 0.0)
        o = ws_qs[C:] + _dot(aqk_ref[pl.ds(r0, C), :], _block_diag(v_new, same_head))
        ms = _dot_data_const(o * o, seg_ones, 2) * (1.0 / HEAD_DIM)
        y = o * lax.rsqrt(ms + EPS) * gain * _silu(z_ref[pl.ds(r0, C), :])
        o_ref[pl.ds(r0, C), :] = y.astype(o_ref.dtype)
        return carry

    lax.fori_loop(0, NC, recur, 0)


def _gdn(pg, conv_w, a_row, dtb_row, norm_row, batch, seq):
    kern = functools.partial(_gdn_kernel, seq=seq)
    C = GDN_CHUNK
    nc = seq // C
    return pl.pallas_call(
        kern,
        grid=(batch,),
        in_specs=[pl.BlockSpec((seq, 3 * BRANCH_WIDTH), lambda b: (b, 0)),
                  pl.BlockSpec((seq, BRANCH_WIDTH), lambda b: (b, 3)),
                  pl.BlockSpec((seq, SMALL_PAD), lambda b: (b, 4 * BRANCH_WIDTH // SMALL_PAD)),
                  _const_spec(conv_w.shape), _const_spec(a_row.shape), _const_spec(dtb_row.shape),
                  _const_spec(norm_row.shape)],
        out_specs=pl.BlockSpec((seq, BRANCH_WIDTH), lambda b: (b, 0)),
        out_shape=jax.ShapeDtypeStruct((batch * seq, BRANCH_WIDTH), BF16),
        scratch_shapes=[pltpu.VMEM((seq, 3 * BRANCH_WIDTH), F32),
                        pltpu.VMEM((seq, BRANCH_WIDTH), F32),
                        pltpu.VMEM((2 * seq, BRANCH_WIDTH), BF16),
                        pltpu.VMEM((seq, BRANCH_WIDTH), BF16),
                        pltpu.VMEM((nc * BRANCH_WIDTH, LANES), BF16),
                        pltpu.VMEM((nc, SUBLANES, BRANCH_WIDTH), F32),
                        pltpu.VMEM((BRANCH_WIDTH, BRANCH_WIDTH), F32)],
        compiler_params=_params(("parallel",)),
        name="gdn",
    )(pg, pg, pg, conv_w, a_row, dtb_row, norm_row)


def _ssd_kernel(xbc_ref, z_ref, dt_ref, convw_ref, convb_ref, alog_ref, dtb_ref, dskip_ref, norm_ref,
                o_ref, conv_ref, s_ref, *, seq):
    L = SSD_CHUNK
    W = BRANCH_WIDTH
    GN = SSD_GROUPS * SSD_STATE
    _conv_silu_into(xbc_ref, convw_ref, convb_ref[...], conv_ref, seq, L)
    s_ref[...] = jnp.zeros_like(s_ref)

    ri = lax.broadcasted_iota(jnp.int32, (L, L), 0)
    ci = lax.broadcasted_iota(jnp.int32, (L, L), 1)
    incl = ri >= ci
    tril = jnp.where(incl, 1.0, 0.0)
    first = lax.broadcasted_iota(jnp.int32, (1, LANES), 1) < HEAD_DIM
    neg_a = -jnp.exp(alog_ref[...])
    dtb = dtb_ref[...]
    dskip = dskip_ref[...]
    gain = norm_ref[...]

    def body(c, carry):
        r0 = pl.multiple_of(c * L, L)
        xbc = conv_ref[pl.ds(r0, L), :]
        zt = z_ref[pl.ds(r0, L), :]
        dt = _softplus(dt_ref[pl.ds(r0, L), :] + dtb)
        acs = _hdot(tril, dt * neg_a)
        acs_rows = acs.T
        a_last = acs[L - 1:L, :]
        e_acs = jnp.exp(acs)
        dt_rem = dt * jnp.exp(a_last - acs)
        e_last = jnp.exp(a_last)
        ys = []
        for g in range(SSD_GROUPS):
            h0, h1 = PAIR * g, PAIR * g + 1
            x2 = xbc[:, g * LANES:(g + 1) * LANES]
            bm = xbc[:, W + g * SSD_STATE:W + (g + 1) * SSD_STATE]
            cm = xbc[:, W + GN + g * SSD_STATE:W + GN + (g + 1) * SSD_STATE]
            cb = _bdot_nt(cm, bm)
            xdt = x2 * jnp.where(first, dt[:, h0:h0 + 1], dt[:, h1:h1 + 1])
            yd = []
            for h in (h0, h1):
                lmat = jnp.exp(jnp.where(incl, acs[:, h:h + 1] - acs_rows[h:h + 1, :], -jnp.inf))
                yd.append(_bdot(cb * lmat, xdt))
            y_diag = jnp.where(first, yd[0], yd[1])
            s = s_ref[g]
            y_off = _bdot(cm, s) * jnp.where(first, e_acs[:, h0:h0 + 1], e_acs[:, h1:h1 + 1])
            x_rem = x2 * jnp.where(first, dt_rem[:, h0:h0 + 1], dt_rem[:, h1:h1 + 1])
            s_ref[g] = (s * jnp.where(first, e_last[:, h0:h0 + 1], e_last[:, h1:h1 + 1])
                        + _bdot(bm.T, x_rem))
            ys.append(y_diag + y_off)
        y = jnp.concatenate(ys, axis=1) + dskip * xbc[:, :W]
        o_ref[pl.ds(r0, L), :] = _rms(y * _silu(zt), gain).astype(o_ref.dtype)
        return carry

    lax.fori_loop(0, seq // L, body, 0)


def _ssd(ps, conv_w, conv_b, a_row, dtb_row, dskip_row, norm_row, batch, seq):
    kern = functools.partial(_ssd_kernel, seq=seq)
    return pl.pallas_call(
        kern,
        grid=(batch,),
        in_specs=[pl.BlockSpec((seq, SSD_CONV_DIM), lambda b: (b, 0)),
                  pl.BlockSpec((seq, BRANCH_WIDTH), lambda b: (b, SSD_CONV_DIM // BRANCH_WIDTH)),
                  pl.BlockSpec((seq, SMALL_PAD), lambda b: (b, (SSD_CONV_DIM + BRANCH_WIDTH) // SMALL_PAD)),
                  _const_spec(conv_w.shape), _const_spec(conv_b.shape), _const_spec(a_row.shape),
                  _const_spec(dtb_row.shape), _const_spec(dskip_row.shape), _const_spec(norm_row.shape)],
        out_specs=pl.BlockSpec((seq, BRANCH_WIDTH), lambda b: (b, 0)),
        out_shape=jax.ShapeDtypeStruct((batch * seq, BRANCH_WIDTH), BF16),
        scratch_shapes=[pltpu.VMEM((seq, SSD_CONV_DIM), F32),
                        pltpu.VMEM((SSD_GROUPS, SSD_STATE, LANES), F32)],
        compiler_params=_params(("parallel",)),
        name="ssd",
    )(ps, ps, ps, conv_w, conv_b, a_row, dtb_row, dskip_row, norm_row)


def _head_masked(q, first_half):
    lane = lax.broadcasted_iota(jnp.int32, (1, LANES), 1)
    keep = (lane < HEAD_DIM) if first_half else (lane >= HEAD_DIM)
    return jnp.where(keep, q.astype(F32), 0.0)


def _moba_kernel(q_ref, k_ref, v_ref, o_ref, kmean_ref, vt_ref, *, seq):
    BS = MOBA_BLOCK
    NB = seq // BS
    W = BRANCH_WIDTH
    qb = pl.program_id(1)
    scale = HEAD_DIM ** -0.5
    q = q_ref[...]

    @pl.when(qb == 0)
    def _():
        kmean = jnp.sum(k_ref[...].astype(F32).reshape(NB, BS, W), axis=1) * (1.0 / BS)
        kmean_ref[...] = jnp.concatenate([kmean, jnp.zeros((SUBLANES, W), F32)], axis=0)
        for c in range(seq // LANES):
            blk = v_ref[c * LANES:(c + 1) * LANES, :].astype(F32)
            vt_ref[:, c * LANES:(c + 1) * LANES] = jnp.concatenate(
                [blk[:, :LANES].T, blk[:, LANES:].T], axis=0).astype(BF16)

    rel = (lax.broadcasted_iota(jnp.int32, (BS, BS), 1)
           - lax.broadcasted_iota(jnp.int32, (BS, BS), 0)).astype(F32)
    first_rows = lax.broadcasted_iota(jnp.int32, (LANES, 1), 0) < HEAD_DIM
    blk_id = lax.broadcasted_iota(jnp.int32, (SUBLANES, BS), 0)
    heads = range(N_HEADS)
    pair_of = [h // PAIR for h in heads]
    lanes_of = lambda x, p: x[:, p * LANES:(p + 1) * LANES]
    slopes = [2.0 ** (-8.0 * (h + 1) / N_HEADS) for h in heads]

    qf = [_head_masked(lanes_of(q, pair_of[h]), h % PAIR == 0) for h in heads]
    km = _split(kmean_ref[...], 3)
    gate = []
    for h in heads:
        qh = qf[h].astype(BF16)
        g_h = None
        for piece in km:
            t = lax.dot_general(lanes_of(piece, pair_of[h]), qh, (((1,), (1,)), ((), ())),
                                preferred_element_type=F32)
            g_h = t if g_h is None else g_h + t
        gate.append(g_h[:NB])
    sels = []
    for h in heads:
        rank = jnp.zeros((NB, BS), F32)
        for i in range(NB):
            gi = gate[h][i:i + 1, :]
            beats = jnp.where(gi > gate[h], 1.0, jnp.where((gi == gate[h]) & (blk_id > i), 1.0, 0.0))
            rank = rank + jnp.where(i < qb, beats, 0.0)
        sels.append(jnp.where((rank < MOBA_TOPK) & (blk_id < qb), 1.0, 0.0))
    qms = [(qf[h] * scale).astype(BF16) for h in heads]

    def block(j, carry, own):
        r0 = pl.multiple_of(j * BS, BS)
        k_j = k_ref[pl.ds(r0, BS), :]
        vt_j = vt_ref[:, pl.ds(r0, BS)]
        dist = rel if own else rel + ((qb - j) * BS).astype(F32)
        m, l, acc = carry[:N_HEADS], carry[N_HEADS:2 * N_HEADS], carry[2 * N_HEADS:]

        raw = [_bdot_nt(lanes_of(k_j, pair_of[h]), qms[h]) for h in heads]
        if own:
            keep = [rel >= 0] * N_HEADS
        else:
            keep = [jnp.sum(jnp.where(blk_id == j, sels[h], 0.0), axis=0, keepdims=True) > 0.5 for h in heads]
        s = [jnp.where(keep[h], raw[h] - slopes[h] * dist, -jnp.inf) for h in heads]
        m_new = [jnp.max(s_, axis=0, keepdims=True) for s_ in s]
        if not own:
            m_new = [jnp.maximum(m[h], m_new[h]) for h in heads]
        p = [jnp.exp(s[h] - m_new[h]) for h in heads]
        pv = [_bdot(vt_j[pair_of[h] * LANES:(pair_of[h] + 1) * LANES, :], p[h]) for h in heads]
        l_new = [jnp.sum(p_, axis=0, keepdims=True) for p_ in p]
        if not own:
            alpha = [jnp.exp(m[h] - m_new[h]) for h in heads]
            l_new = [l[h] * alpha[h] + l_new[h] for h in heads]
            pv = [acc[h] * alpha[h] + pv[h] for h in heads]
        return tuple(m_new) + tuple(l_new) + tuple(pv)

    state = block(qb, (None,) * (3 * N_HEADS), True)
    out = lax.fori_loop(0, qb, lambda j, c: block(j, c, False), state)
    l, acc = out[N_HEADS:2 * N_HEADS], out[2 * N_HEADS:]
    o = [acc[h] / l[h] for h in heads]
    pairs = [jnp.where(first_rows, o[PAIR * p_], o[PAIR * p_ + 1]) for p_ in range(N_HEADS // PAIR)]
    o_ref[...] = jnp.concatenate(
        [jnp.concatenate([t[:, :LANES].T, t[:, LANES:].T], axis=0) for t in pairs], axis=1).astype(o_ref.dtype)


def _moba(pa, batch, seq):
    BS = MOBA_BLOCK
    nq = seq // BS
    assert nq == SUBLANES
    W = BRANCH_WIDTH
    kern = functools.partial(_moba_kernel, seq=seq)
    return pl.pallas_call(
        kern,
        grid=(batch, nq),
        in_specs=[pl.BlockSpec((BS, W), lambda b, i: (b * nq + i, 0)),
                  pl.BlockSpec((seq, W), lambda b, i: (b, 1)),
                  pl.BlockSpec((seq, W), lambda b, i: (b, 2))],
        out_specs=pl.BlockSpec((BS, W), lambda b, i: (b * nq + i, 0)),
        out_shape=jax.ShapeDtypeStruct((batch * seq, W), BF16),
        scratch_shapes=[pltpu.VMEM((2 * SUBLANES, W), F32),
                        pltpu.VMEM((W, seq), BF16)],
        compiler_params=_params(("parallel", "arbitrary")),
        name="moba",
    )(pa, pa, pa)


def _sb_kernel(q_ref, k_ref, v_ref, o_ref, vt_ref, *, seq):
    TQ = SB_TILE
    qb = pl.program_id(1)
    scale = HEAD_DIM ** -0.5
    q = q_ref[...]

    @pl.when(qb == 0)
    def _():
        for c in range(seq // LANES):
            blk = v_ref[c * LANES:(c + 1) * LANES, :].astype(F32)
            vt_ref[:, c * LANES:(c + 1) * LANES] = jnp.concatenate(
                [blk[:, :LANES].T, blk[:, LANES:].T], axis=0).astype(BF16)

    before = lax.broadcasted_iota(jnp.int32, (TQ, TQ), 0) < lax.broadcasted_iota(jnp.int32, (TQ, TQ), 1)
    later = jnp.where(before, 1.0, 0.0).astype(BF16)
    first_rows = lax.broadcasted_iota(jnp.int32, (LANES, 1), 0) < HEAD_DIM
    heads = range(N_HEADS)
    pair_of = [h // PAIR for h in heads]
    lanes_of = lambda x, p: x[:, p * LANES:(p + 1) * LANES]
    qms = [(_head_masked(lanes_of(q, pair_of[h]), h % PAIR == 0) * scale).astype(BF16) for h in heads]

    def tile(kb, carry, diagonal):
        r0 = pl.multiple_of(kb * TQ, TQ)
        k_j = k_ref[pl.ds(r0, TQ), :]
        vt_j = vt_ref[:, pl.ds(r0, TQ)]
        tails, accs = carry[:N_HEADS], carry[N_HEADS:]
        z2 = [_bdot_nt(lanes_of(k_j, pair_of[h]), qms[h]) * LOG2E for h in heads]
        cost = [jnp.maximum(z_, 0.0) + jnp.log2(1.0 + jnp.exp2(-jnp.abs(z_))) for z_ in z2]
        if diagonal:
            cost = [jnp.where(before, c_, 0.0) for c_ in cost]
        after = [_dot_const_data(later, c_, 2) for c_ in cost]
        w = [jnp.exp2(z2[h] - cost[h] - after[h] - tails[h]) for h in heads]
        if diagonal:
            w = [jnp.where(before, w_, 0.0) for w_ in w]
        pv = [_bdot(vt_j[pair_of[h] * LANES:(pair_of[h] + 1) * LANES, :], w[h]) for h in heads]
        accs = [accs[h] + pv[h] for h in heads]
        tails = [tails[h] + jnp.sum(cost[h], axis=0, keepdims=True) for h in heads]
        return tuple(tails) + tuple(accs)

    def live(tails):
        least = tails[0]
        for t_ in tails[1:]:
            least = jnp.minimum(least, t_)
        return jnp.min(least) < SB_DEAD_TAIL

    init = tuple(jnp.zeros((1, TQ), F32) for _ in heads) + tuple(jnp.zeros((LANES, TQ), F32) for _ in heads)
    carry = tile(qb, init, True)

    def step(state):
        i = state[0]
        new = tile(qb - 1 - i, state[2:], False)
        return (i + 1, live(new[:N_HEADS])) + new

    out = lax.while_loop(lambda state: jnp.logical_and(state[0] < qb, state[1]), step,
                         (jnp.int32(0), live(carry[:N_HEADS])) + carry)
    accs = out[2 + N_HEADS:]
    pairs = [jnp.where(first_rows, accs[PAIR * p], accs[PAIR * p + 1]) for p in range(N_HEADS // PAIR)]
    o_ref[...] = jnp.concatenate(
        [jnp.concatenate([t[:, c * LANES:(c + 1) * LANES].T for c in range(TQ // LANES)], axis=0) for t in pairs],
        axis=1).astype(o_ref.dtype)


def _sb(pa, batch, seq):
    TQ = SB_TILE
    nq = seq // TQ
    W = BRANCH_WIDTH
    return pl.pallas_call(
        functools.partial(_sb_kernel, seq=seq),
        grid=(batch, nq),
        in_specs=[pl.BlockSpec((TQ, W), lambda b, i: (b * nq + i, 3)),
                  pl.BlockSpec((seq, W), lambda b, i: (b, 4)),
                  pl.BlockSpec((seq, W), lambda b, i: (b, 5))],
        out_specs=pl.BlockSpec((TQ, W), lambda b, i: (b * nq + i, 0)),
        out_shape=jax.ShapeDtypeStruct((batch * seq, W), BF16),
        scratch_shapes=[pltpu.VMEM((W, seq), BF16)],
        compiler_params=_params(("parallel", "arbitrary")),
        name="sb",
    )(pa, pa, pa)


def _merge_kernel(x_ref, ya_ref, yb_ref, yc_ref, yd_ref, gpre_ref, wgate_ref, wbr_ref, wout_ref, gpost_ref, o_ref):
    x = x_ref[...]
    h = _rms(x, gpre_ref[...]).astype(BF16)
    merged = None
    for g, y_ref in enumerate((ya_ref, yb_ref, yc_ref, yd_ref)):
        gate = _sigmoid(_dot(h, wgate_ref[g]))
        term = gate * _dot(y_ref[...], wbr_ref[g])
        merged = term if merged is None else merged + term
    mix = _dot(merged.astype(BF16), wout_ref[...])
    o_ref[...] = x + _rms(mix, gpost_ref[...])


def _merge(x, ya, yb, yc, yd, gpre, wgate, wbr, wout, gpost):
    n = x.shape[0]
    tm = TOKEN_TILE
    row = lambda w: pl.BlockSpec((tm, w), lambda i: (i, 0))
    return pl.pallas_call(
        _merge_kernel,
        grid=(n // tm,),
        in_specs=[row(D_MODEL)] + [row(BRANCH_WIDTH)] * N_BRANCH
        + [_const_spec(gpre.shape), _const_spec(wgate.shape), _const_spec(wbr.shape),
           _const_spec(wout.shape), _const_spec(gpost.shape)],
        out_specs=row(D_MODEL),
        out_shape=jax.ShapeDtypeStruct((n, D_MODEL), F32),
        compiler_params=_params(("parallel",)),
        name="merge",
    )(x, ya, yb, yc, yd, gpre, wgate, wbr, wout, gpost)


def _ffn_kernel(x_ref, gpre_ref, wup_ref, wdown_ref, gpost_ref, o_ref):
    x = x_ref[...]
    h = _rms(x, gpre_ref[...]).astype(BF16)
    f = None
    for c in range(D_FF // FF_CHUNK):
        u = _dot(h, wup_ref[:, c * FF_CHUNK:(c + 1) * FF_CHUNK])
        a = jnp.square(jnp.maximum(u, 0.0)).astype(BF16)
        t = _dot(a, wdown_ref[c * FF_CHUNK:(c + 1) * FF_CHUNK, :])
        f = t if f is None else f + t
    o_ref[...] = x + _rms(f, gpost_ref[...])


def _ffn(x, gpre, wup, wdown, gpost):
    n = x.shape[0]
    tm = TOKEN_TILE
    row = pl.BlockSpec((tm, D_MODEL), lambda i: (i, 0))
    return pl.pallas_call(
        _ffn_kernel,
        grid=(n // tm,),
        in_specs=[row, _const_spec(gpre.shape), _const_spec(wup.shape), _const_spec(wdown.shape),
                  _const_spec(gpost.shape)],
        out_specs=row,
        out_shape=jax.ShapeDtypeStruct((n, D_MODEL), F32),
        compiler_params=_params(("parallel",)),
        name="ffn",
    )(x, gpre, wup, wdown, gpost)


def _pad_lanes(a, width):
    return jnp.pad(a, ((0, 0), (0, width - a.shape[1])))


def _small_groups(w_in, layer):
    main = 4 * BRANCH_WIDTH
    gdn_in = main + 2 * N_HEADS
    wg_small = _pad_lanes(w_in[layer, :, main:gdn_in], SMALL_PAD)
    ws_small = _pad_lanes(w_in[layer, :, gdn_in + ATT_W + main:], SMALL_PAD)
    return wg_small.astype(BF16), ws_small.astype(BF16)


def kernel(x, norm_mix_pre, norm_mix_post, norm_ffn_pre, norm_ffn_post, w_in, gdn_conv, gdn_a_log, gdn_dt_bias,
           gdn_norm, ssd_conv, ssd_conv_bias, ssd_a_log, ssd_dt_bias, ssd_d, ssd_norm, w_gate, w_branch, w_out,
           w_up, w_down):
    batch, seq, d = x.shape
    assert d == D_MODEL and seq % MOBA_BLOCK == 0 and (batch * seq) % TOKEN_TILE == 0
    depth = w_in.shape[0]
    xf = x.reshape(batch * seq, d).astype(F32)
    row = lambda a: a.reshape(1, -1).astype(F32)
    for l in range(depth):
        wg_small, ws_small = _small_groups(w_in, l)
        pg, pa, ps = _in_proj(xf, row(norm_mix_pre[l]), w_in.astype(F32), l, wg_small, ws_small)
        gdn_a = _pad_lanes(jnp.concatenate([jnp.zeros((N_HEADS,), F32), gdn_a_log[l]]).reshape(1, -1), SMALL_PAD)
        gdn_b = _pad_lanes(jnp.concatenate([jnp.zeros((N_HEADS,), F32), gdn_dt_bias[l]]).reshape(1, -1), SMALL_PAD)
        ya = _gdn(pg, gdn_conv[l].astype(F32), gdn_a, gdn_b, row(jnp.tile(gdn_norm[l], N_HEADS)), batch, seq)
        yb = _moba(pa, batch, seq)
        yc = _sb(pa, batch, seq)
        yd = _ssd(ps, ssd_conv[l].astype(F32), row(ssd_conv_bias[l]),
                  _pad_lanes(row(ssd_a_log[l]), SMALL_PAD), _pad_lanes(row(ssd_dt_bias[l]), SMALL_PAD),
                  row(jnp.repeat(ssd_d[l], HEAD_DIM)), row(ssd_norm[l]), batch, seq)
        xf = _merge(xf, ya, yb, yc, yd, row(norm_mix_pre[l]), w_gate[l].astype(BF16), w_branch[l].astype(BF16),
                    w_out[l].astype(BF16), row(norm_mix_post[l]))
        xf = _ffn(xf, row(norm_ffn_pre[l]), w_up[l].astype(BF16), w_down[l].astype(BF16), row(norm_ffn_post[l]))
    return xf.reshape(batch, seq, d).astype(x.dtype)
```

```python
import functools

import jax
import jax.numpy as jnp
from jax import lax
from jax.experimental import pallas as pl
from jax.experimental.pallas import tpu as pltpu

F32 = jnp.float32
BF16 = jnp.bfloat16
HIGHEST = lax.Precision.HIGHEST

D_MODEL = 1024
N_BRANCH = 4
BRANCH_WIDTH = 256
HEAD_DIM = 64
N_HEADS = 4
CONV_WIDTH = 4
GDN_CHUNK = 64
MOBA_BLOCK = 256
MOBA_TOPK = 3
SSD_STATE = 128
SSD_GROUPS = 2
SSD_CHUNK = 128
D_FF = 4 * D_MODEL
EPS = 1e-6

LANES = 128
SUBLANES = 8
PAIR = LANES // HEAD_DIM
SMALL_PAD = LANES
CONV_HALO = 8
TOKEN_TILE = 512
FF_CHUNK = 1024
SB_TILE = 256
LOG2E = 1.4426950408889634
SB_DEAD_TAIL = 256.0
GDN_GROUP = 2
GDN_UNROLL = 8
VMEM_LIMIT = 56 * 1024 * 1024

GDN_W = 3 * BRANCH_WIDTH + BRANCH_WIDTH + SMALL_PAD
ATT_W = 6 * BRANCH_WIDTH
SSD_CONV_DIM = BRANCH_WIDTH + 2 * SSD_GROUPS * SSD_STATE
SSD_W = SSD_CONV_DIM + BRANCH_WIDTH + SMALL_PAD


def _dot(a, b):
    return jnp.dot(a, b, preferred_element_type=F32)


def _bdot(a, b):
    return _dot(a.astype(BF16), b.astype(BF16))


def _bdot_nt(a, b):
    return lax.dot_general(a.astype(BF16), b.astype(BF16), (((1,), (1,)), ((), ())),
                           preferred_element_type=F32)


def _hdot(a, b):
    return jnp.dot(a, b, preferred_element_type=F32, precision=HIGHEST)


def _split(x, pieces):
    out = []
    for _ in range(pieces - 1):
        p = x.astype(BF16)
        out.append(p)
        x = x - p.astype(F32)
    out.append(x.astype(BF16))
    return out


def _dot_data_const(x, m, pieces):
    acc = None
    for p in _split(x, pieces):
        t = _dot(p, m)
        acc = t if acc is None else acc + t
    return acc


def _dot_const_data(m, x, pieces):
    acc = None
    for p in _split(x, pieces):
        t = _dot(m, p)
        acc = t if acc is None else acc + t
    return acc


def _sigmoid(x):
    return 1.0 / (1.0 + jnp.exp(-x))


def _silu(x):
    return x * _sigmoid(x)


def _softplus(x):
    return jnp.maximum(x, 0.0) + jnp.log(1.0 + jnp.exp(-jnp.abs(x)))


def _rms(x, gain):
    return x * lax.rsqrt(jnp.mean(x * x, axis=-1, keepdims=True) + EPS) * gain


def _params(sem):
    return pltpu.CompilerParams(dimension_semantics=sem, vmem_limit_bytes=VMEM_LIMIT)


def _const_spec(shape):
    nd = len(shape)
    return pl.BlockSpec(shape, lambda *_: (0,) * nd, pipeline_mode=pl.Buffered(1))


def _cols(w_ref, start, width):
    base = start // LANES * LANES
    stop = min(-(-(start + width) // LANES) * LANES, w_ref.shape[1])
    return w_ref[:, base:stop][:, start - base:start - base + width].astype(BF16)


def _in_proj_kernel(x_ref, gain_ref, w_ref, wgs_ref, wss_ref, og_ref, oa_ref, os_ref, wg_ref, wa_ref, ws_ref):
    W = BRANCH_WIDTH
    main = 4 * W

    @pl.when(pl.program_id(0) == 0)
    def _():
        off_att = main + 2 * N_HEADS
        off_ssd = off_att + ATT_W
        for c in range(main // W):
            wg_ref[:, c * W:(c + 1) * W] = _cols(w_ref, c * W, W)
        for c in range(ATT_W // W):
            wa_ref[:, c * W:(c + 1) * W] = _cols(w_ref, off_att + c * W, W)
        for c in range(SSD_CONV_DIM // W):
            ws_ref[:, c * W:(c + 1) * W] = _cols(w_ref, off_ssd + W + c * W, W)
        ws_ref[:, SSD_CONV_DIM:main] = _cols(w_ref, off_ssd, W)

    h = _rms(x_ref[...], gain_ref[...]).astype(BF16)
    og_ref[:, :main] = _dot(h, wg_ref[...])
    og_ref[:, main:] = _dot(h, wgs_ref[...])
    oa_ref[...] = _dot(h, wa_ref[...]).astype(BF16)
    os_ref[:, :main] = _dot(h, ws_ref[...])
    os_ref[:, main:] = _dot(h, wss_ref[...])


def _in_proj(x, gain, w_in, layer, wg_small, ws_small):
    n = x.shape[0]
    tm = TOKEN_TILE
    row = lambda w: pl.BlockSpec((tm, w), lambda i: (i, 0))
    main = 4 * BRANCH_WIDTH
    return pl.pallas_call(
        _in_proj_kernel,
        grid=(n // tm,),
        in_specs=[row(D_MODEL), _const_spec((1, D_MODEL)),
                  pl.BlockSpec((None,) + w_in.shape[1:], lambda i: (layer, 0, 0), pipeline_mode=pl.Buffered(1)),
                  _const_spec(wg_small.shape), _const_spec(ws_small.shape)],
        out_specs=[row(GDN_W), row(ATT_W), row(SSD_W)],
        out_shape=[jax.ShapeDtypeStruct((n, GDN_W), F32), jax.ShapeDtypeStruct((n, ATT_W), BF16),
                   jax.ShapeDtypeStruct((n, SSD_W), F32)],
        scratch_shapes=[pltpu.VMEM((D_MODEL, main), BF16), pltpu.VMEM((D_MODEL, ATT_W), BF16),
                        pltpu.VMEM((D_MODEL, main), BF16)],
        compiler_params=_params(("arbitrary",)),
        name="in_proj",
    )(x, gain, w_in, wg_small, ws_small)


def _conv_silu_into(src_ref, w_ref, bias, dst_ref, seq, chunk):
    w = w_ref[...]
    taps = [w[k:k + 1, :] for k in range(CONV_WIDTH)]

    def finish(acc):
        if bias is not None:
            acc = acc + bias
        return _silu(acc)

    blk = src_ref[0:chunk, :]
    rows = lax.broadcasted_iota(jnp.int32, (chunk, 1), 0)
    acc = taps[CONV_WIDTH - 1] * blk
    for k in range(CONV_WIDTH - 1):
        shift = CONV_WIDTH - 1 - k
        acc = acc + taps[k] * jnp.where(rows >= shift, pltpu.roll(blk, shift, 0), 0.0)
    dst_ref[0:chunk, :] = finish(acc)

    def body(c, carry):
        r0 = pl.multiple_of(c * chunk, chunk)
        blk = src_ref[pl.ds(r0 - CONV_HALO, chunk + CONV_HALO), :]
        acc = taps[CONV_WIDTH - 1] * blk[CONV_HALO:]
        for k in range(CONV_WIDTH - 1):
            acc = acc + taps[k] * pltpu.roll(blk, CONV_WIDTH - 1 - k, 0)[CONV_HALO:]
        dst_ref[pl.ds(r0, chunk), :] = finish(acc)
        return carry

    lax.fori_loop(1, seq // chunk, body, 0)


def _block_diag(x, mask):
    return jnp.where(mask, jnp.concatenate([x] * N_HEADS, axis=0), jnp.zeros((), x.dtype))


def _dot1_bd(a, b, mask):
    return _dot(a.astype(BF16), _block_diag(b.astype(BF16), mask))


def _dot3_bd(a, b, mask):
    ah, al = _split(a, 2)
    bh, bl = _split(b, 2)
    bdh = _block_diag(bh, mask)
    return _dot(ah, bdh) + _dot(al, bdh) + _dot(ah, _block_diag(bl, mask))


def _gdn_prepare_kernel(qkv_ref, bg_ref, convw_ref, alog_ref, dtb_ref,
                        u_ref, wq_ref, aqk_ref, kdt_ref, gl_ref, conv_ref, *, seq):
    C = GDN_CHUNK
    W = BRANCH_WIDTH
    NC = seq // C
    _conv_silu_into(qkv_ref, convw_ref, None, conv_ref, seq, 4 * C)

    def iota(shape, dim):
        return lax.broadcasted_iota(jnp.int32, shape, dim)

    same_head = (iota((W, W), 0) // HEAD_DIM) == (iota((W, W), 1) // HEAD_DIM)
    same_head2 = jnp.concatenate([same_head, same_head], axis=1)
    seg_ones = jnp.where(same_head, 1.0, 0.0).astype(BF16)
    r_c = iota((C, W), 0)
    j_c = iota((C, W), 1) % HEAD_DIM
    incl = r_c >= j_c
    strict = r_c > j_c
    eye = jnp.where(r_c == j_c, 1.0, 0.0)
    blocks = [(r_c // n) == (j_c // n) for n in (SUBLANES, 2 * SUBLANES, 4 * SUBLANES, C)]
    tril = jnp.where(iota((C, C), 0) >= iota((C, C), 1), 1.0, 0.0).astype(BF16)
    ones_c = jnp.ones((C, C), BF16)
    lane = iota((1, LANES), 1)
    ex_r = iota((LANES, 2 * W), 0)
    ex_c = iota((LANES, 2 * W), 1)
    expand = jnp.where(ex_r == jnp.where(ex_c < W, ex_c // HEAD_DIM, N_HEADS + (ex_c - W) // HEAD_DIM),
                       1.0, 0.0).astype(BF16)
    neg_a = -jnp.exp(alog_ref[...])
    dtb = dtb_ref[...]
    scale = HEAD_DIM ** -0.5

    def prepare(i, carry):
        cs = [i * GDN_UNROLL + n for n in range(GDN_UNROLL)]
        r0s = [pl.multiple_of(c * C, C) for c in cs]
        each = lambda f, *xs: [f(*a) for a in zip(*xs)]
        qkv = [conv_ref[pl.ds(r0, C), :] for r0 in r0s]
        q = [a[:, :W] for a in qkv]
        k = [a[:, W:2 * W] for a in qkv]
        v = [a[:, 2 * W:] for a in qkv]
        ss = each(lambda q_, k_: _dot_data_const(jnp.concatenate([q_ * q_, k_ * k_], axis=0), seg_ones, 2), q, k)
        qn = each(lambda q_, s_: q_ * lax.rsqrt(s_[:C] + EPS) * scale, q, ss)
        kn = each(lambda k_, s_: k_ * lax.rsqrt(s_[C:] + EPS), k, ss)
        bg = [bg_ref[pl.ds(r0, C), :] for r0 in r0s]
        gc = each(lambda b_: _dot_const_data(tril, neg_a * _softplus(b_ + dtb), 3), bg)
        ex = each(lambda b_, g_: _dot_data_const(jnp.where(lane < N_HEADS, _sigmoid(b_), g_), expand, 3), bg, gc)
        beta = [a[:, :W] for a in ex]
        gcc = [a[:, W:] for a in ex]
        gcr = each(lambda g_: _dot_const_data(ones_c, g_ * eye, 3), gcc)
        dm = each(lambda a_, b_: jnp.exp(jnp.where(incl, a_ - b_, -jnp.inf)), gcc, gcr)
        kb = each(lambda a_, b_: a_ * b_, kn, beta)
        vb = each(lambda a_, b_: a_ * b_, v, beta)
        qk = each(lambda q_, kb_, kn_: lax.dot_general(
            jnp.concatenate([q_, kb_], axis=0).astype(BF16), _block_diag(kn_.astype(BF16), same_head),
            (((1,), (1,)), ((), ())), preferred_element_type=F32), qn, kb, kn)
        a_qk = each(lambda x_, d_: x_[:C] * d_, qk, dm)
        a = each(lambda x_, d_: jnp.where(strict, x_[C:] * d_, 0.0), qk, dm)
        d = [jnp.where(blocks[0], a_, 0.0) for a_ in a]
        p = [eye - d_ for d_ in d]
        b = each(lambda d_: _dot3_bd(d_, d_, same_head), d)
        pb = each(lambda p_, b_: _dot3_bd(jnp.concatenate([p_, b_], axis=0), b_, same_head), p, b)
        p = each(lambda p_, x_: p_ + x_[:C], p, pb)
        p = each(lambda p_, x_: p_ + _dot3_bd(p_, x_[C:], same_head), p, pb)
        for inner, outer in zip(blocks[:-1], blocks[1:]):
            pe = each(lambda p_, a_: _dot1_bd(p_, jnp.where(outer & ~inner, a_, 0.0), same_head), p, a)
            p = each(lambda p_, x_: p_ - _dot1_bd(x_, p_, same_head), p, pe)
        t_inv = p
        e_gc = [jnp.exp(g_) for g_ in gcc]
        uw = each(lambda t_, vb_, kb_, e_: _dot3_bd(t_, jnp.concatenate([vb_, kb_ * e_], axis=1), same_head2),
                  t_inv, vb, kb, e_gc)
        for n, c in enumerate(cs):
            r0 = r0s[n]
            g_last = gcc[n][C - 1:C, :]
            u_ref[pl.ds(r0, C), :] = uw[n][:, :W]
            wq_ref[pl.ds(pl.multiple_of(c * 2 * C, 2 * C), 2 * C), :] = jnp.concatenate(
                [uw[n][:, W:], qn[n] * e_gc[n]], axis=0).astype(BF16)
            aqk_ref[pl.ds(r0, C), :] = a_qk[n].astype(BF16)
            k_dec = jnp.concatenate([kn[n] * jnp.exp(g_last - gcc[n]), jnp.zeros((LANES - C, W), F32)], axis=0)
            kdt_ref[pl.ds(pl.multiple_of(c * W, W), W), :] = jnp.concatenate(
                [k_dec[:, :LANES].T, k_dec[:, LANES:].T], axis=0).astype(BF16)
            gl_ref[c] = jnp.broadcast_to(jnp.exp(g_last), (SUBLANES, W))
        return carry

    lax.fori_loop(0, NC // GDN_UNROLL, prepare, 0)


def _gdn_recur_kernel(u_ref, wq_ref, aqk_ref, kdt_ref, gl_ref, z_ref, norm_ref, o_ref, s_ref, *, seq, group):
    C = GDN_CHUNK
    W = BRANCH_WIDTH
    NC = seq // C
    same_head = ((lax.broadcasted_iota(jnp.int32, (W, W), 0) // HEAD_DIM)
                 == (lax.broadcasted_iota(jnp.int32, (W, W), 1) // HEAD_DIM))
    seg_ones = jnp.where(same_head, 1.0, 0.0).astype(BF16)
    gain = norm_ref[...]
    seqs = range(group)
    s_ref[...] = jnp.zeros_like(s_ref)

    def recur(c, carry):
        rows = [pl.multiple_of(g * seq + c * C, C) for g in seqs]
        rows2 = [pl.multiple_of(2 * (g * seq + c * C), 2 * C) for g in seqs]
        rows_k = [pl.multiple_of((g * NC + c) * W, W) for g in seqs]
        s = [s_ref[g] for g in seqs]
        ws_qs = [_dot(wq_ref[pl.ds(rows2[g], 2 * C), :], s[g].astype(BF16)) for g in seqs]
        v_new = [(u_ref[pl.ds(rows[g], C), :] - ws_qs[g][:C]).astype(BF16) for g in seqs]
        upd = [_dot(kdt_ref[pl.ds(rows_k[g], W), :],
                    jnp.concatenate([v_new[g], jnp.zeros((LANES - C, W), BF16)], axis=0)) for g in seqs]
        for g in seqs:
            s_ref[g] = s[g] * gl_ref[g * NC + c][0:1, :] + jnp.where(same_head, upd[g], 0.0)
        o = [ws_qs[g][C:] + _dot(aqk_ref[pl.ds(rows[g], C), :], _block_diag(v_new[g], same_head)) for g in seqs]
        ms = [_dot_data_const(o_ * o_, seg_ones, 2) * (1.0 / HEAD_DIM) for o_ in o]
        for g in seqs:
            y = o[g] * lax.rsqrt(ms[g] + EPS) * gain * _silu(z_ref[pl.ds(rows[g], C), :])
            o_ref[pl.ds(rows[g], C), :] = y.astype(o_ref.dtype)
        return carry

    lax.fori_loop(0, NC, recur, 0)


def _gdn(pg, conv_w, a_row, dtb_row, norm_row, batch, seq):
    C = GDN_CHUNK
    W = BRANCH_WIDTH
    nc = seq // C
    n = batch * seq
    u, wq, aqk, kdt, gl = pl.pallas_call(
        functools.partial(_gdn_prepare_kernel, seq=seq),
        grid=(batch,),
        in_specs=[pl.BlockSpec((seq, 3 * W), lambda b: (b, 0)),
                  pl.BlockSpec((seq, SMALL_PAD), lambda b: (b, 4 * W // SMALL_PAD)),
                  _const_spec(conv_w.shape), _const_spec(a_row.shape), _const_spec(dtb_row.shape)],
        out_specs=[pl.BlockSpec((seq, W), lambda b: (b, 0)),
                   pl.BlockSpec((2 * seq, W), lambda b: (b, 0)),
                   pl.BlockSpec((seq, W), lambda b: (b, 0)),
                   pl.BlockSpec((nc * W, LANES), lambda b: (b, 0)),
                   pl.BlockSpec((nc, SUBLANES, W), lambda b: (b, 0, 0))],
        out_shape=[jax.ShapeDtypeStruct((n, W), F32),
                   jax.ShapeDtypeStruct((2 * n, W), BF16),
                   jax.ShapeDtypeStruct((n, W), BF16),
                   jax.ShapeDtypeStruct((batch * nc * W, LANES), BF16),
                   jax.ShapeDtypeStruct((batch * nc, SUBLANES, W), F32)],
        scratch_shapes=[pltpu.VMEM((seq, 3 * W), F32)],
        compiler_params=_params(("parallel",)),
        name="gdn_prepare",
    )(pg, pg, conv_w, a_row, dtb_row)
    group = GDN_GROUP if batch % GDN_GROUP == 0 else 1
    return pl.pallas_call(
        functools.partial(_gdn_recur_kernel, seq=seq, group=group),
        grid=(batch // group,),
        in_specs=[pl.BlockSpec((group * seq, W), lambda b: (b, 0)),
                  pl.BlockSpec((2 * group * seq, W), lambda b: (b, 0)),
                  pl.BlockSpec((group * seq, W), lambda b: (b, 0)),
                  pl.BlockSpec((group * nc * W, LANES), lambda b: (b, 0)),
                  pl.BlockSpec((group * nc, SUBLANES, W), lambda b: (b, 0, 0)),
                  pl.BlockSpec((group * seq, W), lambda b: (b, 3)),
                  _const_spec(norm_row.shape)],
        out_specs=pl.BlockSpec((group * seq, W), lambda b: (b, 0)),
        out_shape=jax.ShapeDtypeStruct((n, W), BF16),
        scratch_shapes=[pltpu.VMEM((group, W, W), F32)],
        compiler_params=_params(("parallel",)),
        name="gdn_recur",
    )(u, wq, aqk, kdt, gl, pg, norm_row)


def _ssd_kernel(xbc_ref, z_ref, dt_ref, convw_ref, convb_ref, alog_ref, dtb_ref, dskip_ref, norm_ref,
                o_ref, conv_ref, s_ref, *, seq):
    L = SSD_CHUNK
    W = BRANCH_WIDTH
    GN = SSD_GROUPS * SSD_STATE
    _conv_silu_into(xbc_ref, convw_ref, convb_ref[...], conv_ref, seq, L)
    s_ref[...] = jnp.zeros_like(s_ref)

    ri = lax.broadcasted_iota(jnp.int32, (L, L), 0)
    ci = lax.broadcasted_iota(jnp.int32, (L, L), 1)
    incl = ri >= ci
    tril = jnp.where(incl, 1.0, 0.0)
    first = lax.broadcasted_iota(jnp.int32, (1, LANES), 1) < HEAD_DIM
    neg_a = -jnp.exp(alog_ref[...])
    dtb = dtb_ref[...]
    dskip = dskip_ref[...]
    gain = norm_ref[...]

    def body(c, carry):
        r0 = pl.multiple_of(c * L, L)
        xbc = conv_ref[pl.ds(r0, L), :]
        zt = z_ref[pl.ds(r0, L), :]
        dt = _softplus(dt_ref[pl.ds(r0, L), :] + dtb)
        acs = _hdot(tril, dt * neg_a)
        acs_rows = acs.T
        a_last = acs[L - 1:L, :]
        e_acs = jnp.exp(acs)
        dt_rem = dt * jnp.exp(a_last - acs)
        e_last = jnp.exp(a_last)
        ys = []
        for g in range(SSD_GROUPS):
            h0, h1 = PAIR * g, PAIR * g + 1
            x2 = xbc[:, g * LANES:(g + 1) * LANES]
            bm = xbc[:, W + g * SSD_STATE:W + (g + 1) * SSD_STATE]
            cm = xbc[:, W + GN + g * SSD_STATE:W + GN + (g + 1) * SSD_STATE]
            cb = _bdot_nt(cm, bm)
            xdt = x2 * jnp.where(first, dt[:, h0:h0 + 1], dt[:, h1:h1 + 1])
            yd = []
            for h in (h0, h1):
                lmat = jnp.exp(jnp.where(incl, acs[:, h:h + 1] - acs_rows[h:h + 1, :], -jnp.inf))
                yd.append(_bdot(cb * lmat, xdt))
            y_diag = jnp.where(first, yd[0], yd[1])
            s = s_ref[g]
            y_off = _bdot(cm, s) * jnp.where(first, e_acs[:, h0:h0 + 1], e_acs[:, h1:h1 + 1])
            x_rem = x2 * jnp.where(first, dt_rem[:, h0:h0 + 1], dt_rem[:, h1:h1 + 1])
            s_ref[g] = (s * jnp.where(first, e_last[:, h0:h0 + 1], e_last[:, h1:h1 + 1])
                        + _bdot(bm.T, x_rem))
            ys.append(y_diag + y_off)
        y = jnp.concatenate(ys, axis=1) + dskip * xbc[:, :W]
        o_ref[pl.ds(r0, L), :] = _rms(y * _silu(zt), gain).astype(o_ref.dtype)
        return carry

    lax.fori_loop(0, seq // L, body, 0)


def _ssd(ps, conv_w, conv_b, a_row, dtb_row, dskip_row, norm_row, batch, seq):
    kern = functools.partial(_ssd_kernel, seq=seq)
    return pl.pallas_call(
        kern,
        grid=(batch,),
        in_specs=[pl.BlockSpec((seq, SSD_CONV_DIM), lambda b: (b, 0)),
                  pl.BlockSpec((seq, BRANCH_WIDTH), lambda b: (b, SSD_CONV_DIM // BRANCH_WIDTH)),
                  pl.BlockSpec((seq, SMALL_PAD), lambda b: (b, (SSD_CONV_DIM + BRANCH_WIDTH) // SMALL_PAD)),
                  _const_spec(conv_w.shape), _const_spec(conv_b.shape), _const_spec(a_row.shape),
                  _const_spec(dtb_row.shape), _const_spec(dskip_row.shape), _const_spec(norm_row.shape)],
        out_specs=pl.BlockSpec((seq, BRANCH_WIDTH), lambda b: (b, 0)),
        out_shape=jax.ShapeDtypeStruct((batch * seq, BRANCH_WIDTH), BF16),
        scratch_shapes=[pltpu.VMEM((seq, SSD_CONV_DIM), F32),
                        pltpu.VMEM((SSD_GROUPS, SSD_STATE, LANES), F32)],
        compiler_params=_params(("parallel",)),
        name="ssd",
    )(ps, ps, ps, conv_w, conv_b, a_row, dtb_row, dskip_row, norm_row)


def _head_masked(q, first_half):
    lane = lax.broadcasted_iota(jnp.int32, (1, LANES), 1)
    keep = (lane < HEAD_DIM) if first_half else (lane >= HEAD_DIM)
    return jnp.where(keep, q.astype(F32), 0.0)


def _moba_kernel(q_ref, k_ref, v_ref, o_ref, kmean_ref, vt_ref, *, seq):
    BS = MOBA_BLOCK
    NB = seq // BS
    W = BRANCH_WIDTH
    qb = pl.program_id(1)
    scale = HEAD_DIM ** -0.5
    q = q_ref[...]

    @pl.when(qb == 0)
    def _():
        kmean = jnp.sum(k_ref[...].astype(F32).reshape(NB, BS, W), axis=1) * (1.0 / BS)
        kmean_ref[...] = jnp.concatenate([kmean, jnp.zeros((SUBLANES, W), F32)], axis=0)
        for c in range(seq // LANES):
            blk = v_ref[c * LANES:(c + 1) * LANES, :].astype(F32)
            vt_ref[:, c * LANES:(c + 1) * LANES] = jnp.concatenate(
                [blk[:, :LANES].T, blk[:, LANES:].T], axis=0).astype(BF16)

    rel = (lax.broadcasted_iota(jnp.int32, (BS, BS), 1)
           - lax.broadcasted_iota(jnp.int32, (BS, BS), 0)).astype(F32)
    first_rows = lax.broadcasted_iota(jnp.int32, (LANES, 1), 0) < HEAD_DIM
    blk_id = lax.broadcasted_iota(jnp.int32, (SUBLANES, BS), 0)
    heads = range(N_HEADS)
    pair_of = [h // PAIR for h in heads]
    lanes_of = lambda x, p: x[:, p * LANES:(p + 1) * LANES]
    slopes = [2.0 ** (-8.0 * (h + 1) / N_HEADS) for h in heads]

    qf = [_head_masked(lanes_of(q, pair_of[h]), h % PAIR == 0) for h in heads]
    km = _split(kmean_ref[...], 3)
    gate = []
    for h in heads:
        qh = qf[h].astype(BF16)
        g_h = None
        for piece in km:
            t = lax.dot_general(lanes_of(piece, pair_of[h]), qh, (((1,), (1,)), ((), ())),
                                preferred_element_type=F32)
            g_h = t if g_h is None else g_h + t
        gate.append(g_h[:NB])
    sels = []
    for h in heads:
        rank = jnp.zeros((NB, BS), F32)
        for i in range(NB):
            gi = gate[h][i:i + 1, :]
            beats = jnp.where(gi > gate[h], 1.0, jnp.where((gi == gate[h]) & (blk_id > i), 1.0, 0.0))
            rank = rank + jnp.where(i < qb, beats, 0.0)
        sels.append(jnp.where((rank < MOBA_TOPK) & (blk_id < qb), 1.0, 0.0))
    qms = [(qf[h] * scale).astype(BF16) for h in heads]

    def blocks(js, carry, own):
        r0 = [pl.multiple_of(j * BS, BS) for j in js]
        k_j = [k_ref[pl.ds(r, BS), :] for r in r0]
        vt_j = [vt_ref[:, pl.ds(r, BS)] for r in r0]
        dist = [rel if own else rel + ((qb - j) * BS).astype(F32) for j in js]
        m, l, acc = carry[:N_HEADS], carry[N_HEADS:2 * N_HEADS], carry[2 * N_HEADS:]
        units = [(b, h) for b in range(len(js)) for h in heads]
        raw = {u: _bdot_nt(lanes_of(k_j[u[0]], pair_of[u[1]]), qms[u[1]]) for u in units}
        if own:
            keep = {u: rel >= 0 for u in units}
        else:
            keep = {(b, h): jnp.sum(jnp.where(blk_id == js[b], sels[h], 0.0), axis=0, keepdims=True) > 0.5
                    for b, h in units}
        s = {(b, h): jnp.where(keep[b, h], raw[b, h] - slopes[h] * dist[b], -jnp.inf) for b, h in units}
        m_new = list(m)
        for b, h in units:
            m_blk = jnp.max(s[b, h], axis=0, keepdims=True)
            m_new[h] = m_blk if m_new[h] is None else jnp.maximum(m_new[h], m_blk)
        p = {(b, h): jnp.exp(s[b, h] - m_new[h]) for b, h in units}
        pv, l_new = [None] * N_HEADS, [None] * N_HEADS
        for b, h in units:
            t = _bdot(vt_j[b][pair_of[h] * LANES:(pair_of[h] + 1) * LANES, :], p[b, h])
            pv[h] = t if pv[h] is None else pv[h] + t
            t = jnp.sum(p[b, h], axis=0, keepdims=True)
            l_new[h] = t if l_new[h] is None else l_new[h] + t
        if not own:
            alpha = [jnp.exp(m[h] - m_new[h]) for h in heads]
            l_new = [l[h] * alpha[h] + l_new[h] for h in heads]
            pv = [acc[h] * alpha[h] + pv[h] for h in heads]
        return tuple(m_new) + tuple(l_new) + tuple(pv)

    state = blocks([qb], (None,) * (3 * N_HEADS), True)
    out = lax.fori_loop(0, qb // 2, lambda i, c: blocks([2 * i, 2 * i + 1], c, False), state)
    out = lax.cond(qb % 2 == 1, lambda c: blocks([qb - 1], c, False), lambda c: c, out)
    l, acc = out[N_HEADS:2 * N_HEADS], out[2 * N_HEADS:]
    o = [acc[h] / l[h] for h in heads]
    pairs = [jnp.where(first_rows, o[PAIR * p_], o[PAIR * p_ + 1]) for p_ in range(N_HEADS // PAIR)]
    o_ref[...] = jnp.concatenate(
        [jnp.concatenate([t[:, :LANES].T, t[:, LANES:].T], axis=0) for t in pairs], axis=1).astype(o_ref.dtype)


def _moba(pa, batch, seq):
    BS = MOBA_BLOCK
    nq = seq // BS
    assert nq == SUBLANES
    W = BRANCH_WIDTH
    kern = functools.partial(_moba_kernel, seq=seq)
    return pl.pallas_call(
        kern,
        grid=(batch, nq),
        in_specs=[pl.BlockSpec((BS, W), lambda b, i: (b * nq + i, 0)),
                  pl.BlockSpec((seq, W), lambda b, i: (b, 1)),
                  pl.BlockSpec((seq, W), lambda b, i: (b, 2))],
        out_specs=pl.BlockSpec((BS, W), lambda b, i: (b * nq + i, 0)),
        out_shape=jax.ShapeDtypeStruct((batch * seq, W), BF16),
        scratch_shapes=[pltpu.VMEM((2 * SUBLANES, W), F32),
                        pltpu.VMEM((W, seq), BF16)],
        compiler_params=_params(("parallel", "arbitrary")),
        name="moba",
    )(pa, pa, pa)


def _sb_kernel(q_ref, k_ref, v_ref, o_ref, vt_ref, *, seq):
    TQ = SB_TILE
    qb = pl.program_id(1)
    scale = HEAD_DIM ** -0.5
    q = q_ref[...]

    @pl.when(qb == 0)
    def _():
        for c in range(seq // LANES):
            blk = v_ref[c * LANES:(c + 1) * LANES, :].astype(F32)
            vt_ref[:, c * LANES:(c + 1) * LANES] = jnp.concatenate(
                [blk[:, :LANES].T, blk[:, LANES:].T], axis=0).astype(BF16)

    before = lax.broadcasted_iota(jnp.int32, (TQ, TQ), 0) < lax.broadcasted_iota(jnp.int32, (TQ, TQ), 1)
    later = jnp.where(before, 1.0, 0.0).astype(BF16)
    first_rows = lax.broadcasted_iota(jnp.int32, (LANES, 1), 0) < HEAD_DIM
    heads = range(N_HEADS)
    pair_of = [h // PAIR for h in heads]
    lanes_of = lambda x, p: x[:, p * LANES:(p + 1) * LANES]
    qms = [(_head_masked(lanes_of(q, pair_of[h]), h % PAIR == 0) * scale).astype(BF16) for h in heads]

    def tile(kb, carry, diagonal):
        r0 = pl.multiple_of(kb * TQ, TQ)
        k_j = k_ref[pl.ds(r0, TQ), :]
        vt_j = vt_ref[:, pl.ds(r0, TQ)]
        tails, accs = carry[:N_HEADS], carry[N_HEADS:]
        z2 = [_bdot_nt(lanes_of(k_j, pair_of[h]), qms[h]) * LOG2E for h in heads]
        cost = [jnp.maximum(z_, 0.0) + jnp.log2(1.0 + jnp.exp2(-jnp.abs(z_))) for z_ in z2]
        if diagonal:
            cost = [jnp.where(before, c_, 0.0) for c_ in cost]
        after = [_dot_const_data(later, c_, 2) for c_ in cost]
        w = [jnp.exp2(z2[h] - cost[h] - after[h] - tails[h]) for h in heads]
        if diagonal:
            w = [jnp.where(before, w_, 0.0) for w_ in w]
        pv = [_bdot(vt_j[pair_of[h] * LANES:(pair_of[h] + 1) * LANES, :], w[h]) for h in heads]
        accs = [accs[h] + pv[h] for h in heads]
        tails = [tails[h] + jnp.sum(cost[h], axis=0, keepdims=True) for h in heads]
        return tuple(tails) + tuple(accs)

    def live(tails):
        least = tails[0]
        for t_ in tails[1:]:
            least = jnp.minimum(least, t_)
        return jnp.min(least) < SB_DEAD_TAIL

    init = tuple(jnp.zeros((1, TQ), F32) for _ in heads) + tuple(jnp.zeros((LANES, TQ), F32) for _ in heads)
    carry = tile(qb, init, True)

    def step(state):
        i = state[0]
        new = tile(qb - 1 - i, state[2:], False)
        return (i + 1, live(new[:N_HEADS])) + new

    out = lax.while_loop(lambda state: jnp.logical_and(state[0] < qb, state[1]), step,
                         (jnp.int32(0), live(carry[:N_HEADS])) + carry)
    accs = out[2 + N_HEADS:]
    pairs = [jnp.where(first_rows, accs[PAIR * p], accs[PAIR * p + 1]) for p in range(N_HEADS // PAIR)]
    o_ref[...] = jnp.concatenate(
        [jnp.concatenate([t[:, c * LANES:(c + 1) * LANES].T for c in range(TQ // LANES)], axis=0) for t in pairs],
        axis=1).astype(o_ref.dtype)


def _sb(pa, batch, seq):
    TQ = SB_TILE
    nq = seq // TQ
    W = BRANCH_WIDTH
    return pl.pallas_call(
        functools.partial(_sb_kernel, seq=seq),
        grid=(batch, nq),
        in_specs=[pl.BlockSpec((TQ, W), lambda b, i: (b * nq + i, 3)),
                  pl.BlockSpec((seq, W), lambda b, i: (b, 4)),
                  pl.BlockSpec((seq, W), lambda b, i: (b, 5))],
        out_specs=pl.BlockSpec((TQ, W), lambda b, i: (b * nq + i, 0)),
        out_shape=jax.ShapeDtypeStruct((batch * seq, W), BF16),
        scratch_shapes=[pltpu.VMEM((W, seq), BF16)],
        compiler_params=_params(("parallel", "arbitrary")),
        name="sb",
    )(pa, pa, pa)


def _merge_kernel(x_ref, ya_ref, yb_ref, yc_ref, yd_ref, gpre_ref, wgate_ref, wbr_ref, wout_ref, gpost_ref, o_ref):
    x = x_ref[...]
    h = _rms(x, gpre_ref[...]).astype(BF16)
    merged = None
    for g, y_ref in enumerate((ya_ref, yb_ref, yc_ref, yd_ref)):
        gate = _sigmoid(_dot(h, wgate_ref[g]))
        term = gate * _dot(y_ref[...], wbr_ref[g])
        merged = term if merged is None else merged + term
    mix = _dot(merged.astype(BF16), wout_ref[...])
    o_ref[...] = x + _rms(mix, gpost_ref[...])


def _merge(x, ya, yb, yc, yd, gpre, wgate, wbr, wout, gpost):
    n = x.shape[0]
    tm = TOKEN_TILE
    row = lambda w: pl.BlockSpec((tm, w), lambda i: (i, 0))
    return pl.pallas_call(
        _merge_kernel,
        grid=(n // tm,),
        in_specs=[row(D_MODEL)] + [row(BRANCH_WIDTH)] * N_BRANCH
        + [_const_spec(gpre.shape), _const_spec(wgate.shape), _const_spec(wbr.shape),
           _const_spec(wout.shape), _const_spec(gpost.shape)],
        out_specs=row(D_MODEL),
        out_shape=jax.ShapeDtypeStruct((n, D_MODEL), F32),
        compiler_params=_params(("parallel",)),
        name="merge",
    )(x, ya, yb, yc, yd, gpre, wgate, wbr, wout, gpost)


def _ffn_kernel(x_ref, gpre_ref, wup_ref, wdown_ref, gpost_ref, o_ref):
    x = x_ref[...]
    h = _rms(x, gpre_ref[...]).astype(BF16)
    f = None
    for c in range(D_FF // FF_CHUNK):
        u = _dot(h, wup_ref[:, c * FF_CHUNK:(c + 1) * FF_CHUNK])
        a = jnp.square(jnp.maximum(u, 0.0)).astype(BF16)
        t = _dot(a, wdown_ref[c * FF_CHUNK:(c + 1) * FF_CHUNK, :])
        f = t if f is None else f + t
    o_ref[...] = x + _rms(f, gpost_ref[...])


def _ffn(x, gpre, wup, wdown, gpost):
    n = x.shape[0]
    tm = TOKEN_TILE
    row = pl.BlockSpec((tm, D_MODEL), lambda i: (i, 0))
    return pl.pallas_call(
        _ffn_kernel,
        grid=(n // tm,),
        in_specs=[row, _const_spec(gpre.shape), _const_spec(wup.shape), _const_spec(wdown.shape),
                  _const_spec(gpost.shape)],
        out_specs=row,
        out_shape=jax.ShapeDtypeStruct((n, D_MODEL), F32),
        compiler_params=_params(("parallel",)),
        name="ffn",
    )(x, gpre, wup, wdown, gpost)


def _pad_lanes(a, width):
    return jnp.pad(a, ((0, 0), (0, width - a.shape[1])))


def _small_groups(w_in, layer):
    main = 4 * BRANCH_WIDTH
    gdn_in = main + 2 * N_HEADS
    wg_small = _pad_lanes(w_in[layer, :, main:gdn_in], SMALL_PAD)
    ws_small = _pad_lanes(w_in[layer, :, gdn_in + ATT_W + main:], SMALL_PAD)
    return wg_small.astype(BF16), ws_small.astype(BF16)


def kernel(x, norm_mix_pre, norm_mix_post, norm_ffn_pre, norm_ffn_post, w_in, gdn_conv, gdn_a_log, gdn_dt_bias,
           gdn_norm, ssd_conv, ssd_conv_bias, ssd_a_log, ssd_dt_bias, ssd_d, ssd_norm, w_gate, w_branch, w_out,
           w_up, w_down):
    batch, seq, d = x.shape
    assert d == D_MODEL and seq % MOBA_BLOCK == 0 and (batch * seq) % TOKEN_TILE == 0
    depth = w_in.shape[0]
    xf = x.reshape(batch * seq, d).astype(F32)
    row = lambda a: a.reshape(1, -1).astype(F32)
    for l in range(depth):
        wg_small, ws_small = _small_groups(w_in, l)
        pg, pa, ps = _in_proj(xf, row(norm_mix_pre[l]), w_in.astype(F32), l, wg_small, ws_small)
        gdn_a = _pad_lanes(jnp.concatenate([jnp.zeros((N_HEADS,), F32), gdn_a_log[l]]).reshape(1, -1), SMALL_PAD)
        gdn_b = _pad_lanes(jnp.concatenate([jnp.zeros((N_HEADS,), F32), gdn_dt_bias[l]]).reshape(1, -1), SMALL_PAD)
        ya = _gdn(pg, gdn_conv[l].astype(F32), gdn_a, gdn_b, row(jnp.tile(gdn_norm[l], N_HEADS)), batch, seq)
        yb = _moba(pa, batch, seq)
        yc = _sb(pa, batch, seq)
        yd = _ssd(ps, ssd_conv[l].astype(F32), row(ssd_conv_bias[l]),
                  _pad_lanes(row(ssd_a_log[l]), SMALL_PAD), _pad_lanes(row(ssd_dt_bias[l]), SMALL_PAD),
                  row(jnp.repeat(ssd_d[l], HEAD_DIM)), row(ssd_norm[l]), batch, seq)
        xf = _merge(xf, ya, yb, yc, yd, row(norm_mix_pre[l]), w_gate[l].astype(BF16), w_branch[l].astype(BF16),
                    w_out[l].astype(BF16), row(norm_mix_post[l]))
        xf = _ffn(xf, row(norm_ffn_pre[l]), w_up[l].astype(BF16), w_down[l].astype(BF16), row(norm_ffn_post[l]))
    return xf.reshape(batch, seq, d).astype(x.dtype)
```

```python
import functools

import jax
import jax.numpy as jnp
from jax import lax
from jax.experimental import pallas as pl
from jax.experimental.pallas import tpu as pltpu

F32 = jnp.float32
BF16 = jnp.bfloat16
HIGHEST = lax.Precision.HIGHEST

D_MODEL = 1024
N_BRANCH = 4
BRANCH_WIDTH = 256
HEAD_DIM = 64
N_HEADS = 4
CONV_WIDTH = 4
GDN_CHUNK = 64
MOBA_BLOCK = 256
MOBA_TOPK = 3
SSD_STATE = 128
SSD_GROUPS = 2
SSD_CHUNK = 128
D_FF = 4 * D_MODEL
EPS = 1e-6

LANES = 128
SUBLANES = 8
PAIR = LANES // HEAD_DIM
SMALL_PAD = LANES
CONV_HALO = 8
TOKEN_TILE = 512
TOKEN_SPLIT = 2
FF_CHUNK = 1024
SB_TILE = 256
LOG2E = 1.4426950408889634
SB_DEAD_TAIL = 256.0
SSD_UNROLL = 4
GDN_GROUP = 2
GDN_UNROLL = 8
VMEM_LIMIT = 56 * 1024 * 1024

GDN_W = 3 * BRANCH_WIDTH + BRANCH_WIDTH + SMALL_PAD
ATT_W = 6 * BRANCH_WIDTH
SSD_CONV_DIM = BRANCH_WIDTH + 2 * SSD_GROUPS * SSD_STATE
SSD_W = SSD_CONV_DIM + BRANCH_WIDTH + SMALL_PAD


def _dot(a, b):
    return jnp.dot(a, b, preferred_element_type=F32)


def _bdot(a, b):
    return _dot(a.astype(BF16), b.astype(BF16))


def _bdot_nt(a, b):
    return lax.dot_general(a.astype(BF16), b.astype(BF16), (((1,), (1,)), ((), ())),
                           preferred_element_type=F32)


def _hdot(a, b):
    return jnp.dot(a, b, preferred_element_type=F32, precision=HIGHEST)


def _split(x, pieces):
    out = []
    for _ in range(pieces - 1):
        p = x.astype(BF16)
        out.append(p)
        x = x - p.astype(F32)
    out.append(x.astype(BF16))
    return out


def _dot_data_const(x, m, pieces):
    acc = None
    for p in _split(x, pieces):
        t = _dot(p, m)
        acc = t if acc is None else acc + t
    return acc


def _dot_const_data(m, x, pieces):
    acc = None
    for p in _split(x, pieces):
        t = _dot(m, p)
        acc = t if acc is None else acc + t
    return acc


def _sigmoid(x):
    return 1.0 / (1.0 + jnp.exp(-x))


def _silu(x):
    return x * _sigmoid(x)


def _softplus(x):
    return jnp.maximum(x, 0.0) + jnp.log(1.0 + jnp.exp(-jnp.abs(x)))


def _rms(x, gain):
    return x * lax.rsqrt(jnp.mean(x * x, axis=-1, keepdims=True) + EPS) * gain


def _params(sem):
    return pltpu.CompilerParams(dimension_semantics=sem, vmem_limit_bytes=VMEM_LIMIT)


def _const_spec(shape):
    nd = len(shape)
    return pl.BlockSpec(shape, lambda *_: (0,) * nd, pipeline_mode=pl.Buffered(1))


def _cols(w_ref, start, width):
    base = start // LANES * LANES
    stop = min(-(-(start + width) // LANES) * LANES, w_ref.shape[1])
    return w_ref[:, base:stop][:, start - base:start - base + width].astype(BF16)


def _in_proj_kernel(x_ref, gain_ref, w_ref, wgs_ref, wss_ref, og_ref, oa_ref, os_ref, wg_ref, wa_ref, ws_ref):
    W = BRANCH_WIDTH
    main = 4 * W

    @pl.when(pl.program_id(0) == 0)
    def _():
        off_att = main + 2 * N_HEADS
        off_ssd = off_att + ATT_W
        for c in range(main // W):
            wg_ref[:, c * W:(c + 1) * W] = _cols(w_ref, c * W, W)
        for c in range(ATT_W // W):
            wa_ref[:, c * W:(c + 1) * W] = _cols(w_ref, off_att + c * W, W)
        for c in range(SSD_CONV_DIM // W):
            ws_ref[:, c * W:(c + 1) * W] = _cols(w_ref, off_ssd + W + c * W, W)
        ws_ref[:, SSD_CONV_DIM:main] = _cols(w_ref, off_ssd, W)

    h = _rms(x_ref[...], gain_ref[...]).astype(BF16)
    og_ref[:, :main] = _dot(h, wg_ref[...])
    og_ref[:, main:] = _dot(h, wgs_ref[...])
    oa_ref[...] = _dot(h, wa_ref[...]).astype(BF16)
    os_ref[:, :main] = _dot(h, ws_ref[...])
    os_ref[:, main:] = _dot(h, wss_ref[...])


def _in_proj(x, gain, w_in, layer, wg_small, ws_small):
    n = x.shape[0]
    tm = TOKEN_TILE
    row = lambda w: pl.BlockSpec((tm, w), lambda i: (i, 0))
    main = 4 * BRANCH_WIDTH
    return pl.pallas_call(
        _in_proj_kernel,
        grid=(n // tm,),
        in_specs=[row(D_MODEL), _const_spec((1, D_MODEL)),
                  pl.BlockSpec((None,) + w_in.shape[1:], lambda i: (layer, 0, 0), pipeline_mode=pl.Buffered(1)),
                  _const_spec(wg_small.shape), _const_spec(ws_small.shape)],
        out_specs=[row(GDN_W), row(ATT_W), row(SSD_W)],
        out_shape=[jax.ShapeDtypeStruct((n, GDN_W), F32), jax.ShapeDtypeStruct((n, ATT_W), BF16),
                   jax.ShapeDtypeStruct((n, SSD_W), F32)],
        scratch_shapes=[pltpu.VMEM((D_MODEL, main), BF16), pltpu.VMEM((D_MODEL, ATT_W), BF16),
                        pltpu.VMEM((D_MODEL, main), BF16)],
        compiler_params=_params(("arbitrary",)),
        name="in_proj",
    )(x, gain, w_in, wg_small, ws_small)


def _conv_silu_into(src_ref, w_ref, bias, dst_ref, seq, chunk):
    w = w_ref[...]
    taps = [w[k:k + 1, :] for k in range(CONV_WIDTH)]

    def finish(acc):
        if bias is not None:
            acc = acc + bias
        return _silu(acc)

    blk = src_ref[0:chunk, :]
    rows = lax.broadcasted_iota(jnp.int32, (chunk, 1), 0)
    acc = taps[CONV_WIDTH - 1] * blk
    for k in range(CONV_WIDTH - 1):
        shift = CONV_WIDTH - 1 - k
        acc = acc + taps[k] * jnp.where(rows >= shift, pltpu.roll(blk, shift, 0), 0.0)
    dst_ref[0:chunk, :] = finish(acc)

    def body(c, carry):
        r0 = pl.multiple_of(c * chunk, chunk)
        blk = src_ref[pl.ds(r0 - CONV_HALO, chunk + CONV_HALO), :]
        acc = taps[CONV_WIDTH - 1] * blk[CONV_HALO:]
        for k in range(CONV_WIDTH - 1):
            acc = acc + taps[k] * pltpu.roll(blk, CONV_WIDTH - 1 - k, 0)[CONV_HALO:]
        dst_ref[pl.ds(r0, chunk), :] = finish(acc)
        return carry

    lax.fori_loop(1, seq // chunk, body, 0)


def _block_diag(x, mask):
    return jnp.where(mask, jnp.concatenate([x] * N_HEADS, axis=0), jnp.zeros((), x.dtype))


def _dot1_bd(a, b, mask):
    return _dot(a.astype(BF16), _block_diag(b.astype(BF16), mask))


def _dot3_bd(a, b, mask):
    ah, al = _split(a, 2)
    bh, bl = _split(b, 2)
    bdh = _block_diag(bh, mask)
    return _dot(ah, bdh) + _dot(al, bdh) + _dot(ah, _block_diag(bl, mask))


def _gdn_prepare_kernel(qkv_ref, bg_ref, convw_ref, alog_ref, dtb_ref,
                        u_ref, wq_ref, aqk_ref, kdt_ref, gl_ref, conv_ref, *, seq):
    C = GDN_CHUNK
    W = BRANCH_WIDTH
    NC = seq // C
    _conv_silu_into(qkv_ref, convw_ref, None, conv_ref, seq, 4 * C)

    def iota(shape, dim):
        return lax.broadcasted_iota(jnp.int32, shape, dim)

    same_head = (iota((W, W), 0) // HEAD_DIM) == (iota((W, W), 1) // HEAD_DIM)
    same_head2 = jnp.concatenate([same_head, same_head], axis=1)
    seg_ones = jnp.where(same_head, 1.0, 0.0).astype(BF16)
    r_c = iota((C, W), 0)
    j_c = iota((C, W), 1) % HEAD_DIM
    incl = r_c >= j_c
    strict = r_c > j_c
    eye = jnp.where(r_c == j_c, 1.0, 0.0)
    blocks = [(r_c // n) == (j_c // n) for n in (SUBLANES, 2 * SUBLANES, 4 * SUBLANES, C)]
    tril = jnp.where(iota((C, C), 0) >= iota((C, C), 1), 1.0, 0.0).astype(BF16)
    lane = iota((1, LANES), 1)
    ex_r = iota((LANES, 2 * W), 0)
    ex_c = iota((LANES, 2 * W), 1)
    expand = jnp.where(ex_r == jnp.where(ex_c < W, ex_c // HEAD_DIM, N_HEADS + (ex_c - W) // HEAD_DIM),
                       1.0, 0.0).astype(BF16)
    neg_a = -jnp.exp(alog_ref[...])
    dtb = dtb_ref[...]
    scale = HEAD_DIM ** -0.5

    def row_form(gc):
        t = jnp.concatenate([gc, jnp.zeros((LANES - C, LANES), F32)], axis=0).T
        heads_rows = [t[N_HEADS + h:N_HEADS + h + 1, :] for h in range(N_HEADS)]
        tiles = [jnp.where(lane < HEAD_DIM, heads_rows[PAIR * p_], pltpu.roll(heads_rows[PAIR * p_ + 1], HEAD_DIM, 1))
                 for p_ in range(N_HEADS // PAIR)]
        return jnp.broadcast_to(jnp.concatenate(tiles, axis=1), (C, W))

    def prepare(i, carry):
        cs = [i * GDN_UNROLL + n for n in range(GDN_UNROLL)]
        r0s = [pl.multiple_of(c * C, C) for c in cs]
        each = lambda f, *xs: [f(*a) for a in zip(*xs)]
        qkv = [conv_ref[pl.ds(r0, C), :] for r0 in r0s]
        q = [a[:, :W] for a in qkv]
        k = [a[:, W:2 * W] for a in qkv]
        v = [a[:, 2 * W:] for a in qkv]
        ss = each(lambda q_, k_: _dot_data_const(jnp.concatenate([q_ * q_, k_ * k_], axis=0), seg_ones, 2), q, k)
        qn = each(lambda q_, s_: q_ * lax.rsqrt(s_[:C] + EPS) * scale, q, ss)
        kn = each(lambda k_, s_: k_ * lax.rsqrt(s_[C:] + EPS), k, ss)
        bg = [bg_ref[pl.ds(r0, C), :] for r0 in r0s]
        gc = each(lambda b_: _dot_const_data(tril, neg_a * _softplus(b_ + dtb), 3), bg)
        ex = each(lambda b_, g_: _dot_data_const(jnp.where(lane < N_HEADS, _sigmoid(b_), g_), expand, 3), bg, gc)
        beta = [a[:, :W] for a in ex]
        gcc = [a[:, W:] for a in ex]
        gcr = each(row_form, gc)
        dm = each(lambda a_, b_: jnp.exp(jnp.where(incl, a_ - b_, -jnp.inf)), gcc, gcr)
        kb = each(lambda a_, b_: a_ * b_, kn, beta)
        vb = each(lambda a_, b_: a_ * b_, v, beta)
        qk = each(lambda q_, kb_, kn_: lax.dot_general(
            jnp.concatenate([q_, kb_], axis=0).astype(BF16), _block_diag(kn_.astype(BF16), same_head),
            (((1,), (1,)), ((), ())), preferred_element_type=F32), qn, kb, kn)
        a_qk = each(lambda x_, d_: x_[:C] * d_, qk, dm)
        a = each(lambda x_, d_: jnp.where(strict, x_[C:] * d_, 0.0), qk, dm)
        d = [jnp.where(blocks[0], a_, 0.0) for a_ in a]
        p = [eye - d_ for d_ in d]
        b = each(lambda d_: _dot3_bd(d_, d_, same_head), d)
        pb = each(lambda p_, b_: _dot3_bd(jnp.concatenate([p_, b_], axis=0), b_, same_head), p, b)
        p = each(lambda p_, x_: p_ + x_[:C], p, pb)
        p = each(lambda p_, x_: p_ + _dot3_bd(p_, x_[C:], same_head), p, pb)
        for inner, outer in zip(blocks[:-1], blocks[1:]):
            pe = each(lambda p_, a_: _dot1_bd(p_, jnp.where(outer & ~inner, a_, 0.0), same_head), p, a)
            p = each(lambda p_, x_: p_ - _dot1_bd(x_, p_, same_head), p, pe)
        t_inv = p
        e_gc = [jnp.exp(g_) for g_ in gcc]
        uw = each(lambda t_, vb_, kb_, e_: _dot3_bd(t_, jnp.concatenate([vb_, kb_ * e_], axis=1), same_head2),
                  t_inv, vb, kb, e_gc)
        for n, c in enumerate(cs):
            r0 = r0s[n]
            g_last = gcc[n][C - 1:C, :]
            u_ref[pl.ds(r0, C), :] = uw[n][:, :W]
            wq_ref[pl.ds(pl.multiple_of(c * 2 * C, 2 * C), 2 * C), :] = jnp.concatenate(
                [uw[n][:, W:], qn[n] * e_gc[n]], axis=0).astype(BF16)
            aqk_ref[pl.ds(r0, C), :] = a_qk[n].astype(BF16)
            k_dec = jnp.concatenate([kn[n] * jnp.exp(g_last - gcc[n]), jnp.zeros((LANES - C, W), F32)], axis=0)
            kdt_ref[pl.ds(pl.multiple_of(c * W, W), W), :] = jnp.concatenate(
                [k_dec[:, :LANES].T, k_dec[:, LANES:].T], axis=0).astype(BF16)
            gl_ref[c] = jnp.broadcast_to(jnp.exp(g_last), (SUBLANES, W))
        return carry

    lax.fori_loop(0, NC // GDN_UNROLL, prepare, 0)


def _gdn_recur_kernel(u_ref, wq_ref, aqk_ref, kdt_ref, gl_ref, z_ref, norm_ref, o_ref, s_ref, *, seq, group):
    C = GDN_CHUNK
    W = BRANCH_WIDTH
    NC = seq // C
    same_head = ((lax.broadcasted_iota(jnp.int32, (W, W), 0) // HEAD_DIM)
                 == (lax.broadcasted_iota(jnp.int32, (W, W), 1) // HEAD_DIM))
    seg_ones = jnp.where(same_head, 1.0, 0.0).astype(BF16)
    gain = norm_ref[...]
    seqs = range(group)
    s_ref[...] = jnp.zeros_like(s_ref)

    def recur(c, carry):
        rows = [pl.multiple_of(g * seq + c * C, C) for g in seqs]
        rows2 = [pl.multiple_of(2 * (g * seq + c * C), 2 * C) for g in seqs]
        rows_k = [pl.multiple_of((g * NC + c) * W, W) for g in seqs]
        s = [s_ref[g] for g in seqs]
        ws_qs = [_dot(wq_ref[pl.ds(rows2[g], 2 * C), :], s[g].astype(BF16)) for g in seqs]
        v_new = [(u_ref[pl.ds(rows[g], C), :] - ws_qs[g][:C]).astype(BF16) for g in seqs]
        upd = [_dot(kdt_ref[pl.ds(rows_k[g], W), :],
                    jnp.concatenate([v_new[g], jnp.zeros((LANES - C, W), BF16)], axis=0)) for g in seqs]
        for g in seqs:
            s_ref[g] = s[g] * gl_ref[g * NC + c][0:1, :] + jnp.where(same_head, upd[g], 0.0)
        o = [ws_qs[g][C:] + _dot(aqk_ref[pl.ds(rows[g], C), :], _block_diag(v_new[g], same_head)) for g in seqs]
        ms = [_dot_data_const(o_ * o_, seg_ones, 2) * (1.0 / HEAD_DIM) for o_ in o]
        for g in seqs:
            y = o[g] * lax.rsqrt(ms[g] + EPS) * gain * _silu(z_ref[pl.ds(rows[g], C), :])
            o_ref[pl.ds(rows[g], C), :] = y.astype(o_ref.dtype)
        return carry

    lax.fori_loop(0, NC, recur, 0)


def _gdn(pg, conv_w, a_row, dtb_row, norm_row, batch, seq):
    C = GDN_CHUNK
    W = BRANCH_WIDTH
    nc = seq // C
    n = batch * seq
    u, wq, aqk, kdt, gl = pl.pallas_call(
        functools.partial(_gdn_prepare_kernel, seq=seq),
        grid=(batch,),
        in_specs=[pl.BlockSpec((seq, 3 * W), lambda b: (b, 0)),
                  pl.BlockSpec((seq, SMALL_PAD), lambda b: (b, 4 * W // SMALL_PAD)),
                  _const_spec(conv_w.shape), _const_spec(a_row.shape), _const_spec(dtb_row.shape)],
        out_specs=[pl.BlockSpec((seq, W), lambda b: (b, 0)),
                   pl.BlockSpec((2 * seq, W), lambda b: (b, 0)),
                   pl.BlockSpec((seq, W), lambda b: (b, 0)),
                   pl.BlockSpec((nc * W, LANES), lambda b: (b, 0)),
                   pl.BlockSpec((nc, SUBLANES, W), lambda b: (b, 0, 0))],
        out_shape=[jax.ShapeDtypeStruct((n, W), F32),
                   jax.ShapeDtypeStruct((2 * n, W), BF16),
                   jax.ShapeDtypeStruct((n, W), BF16),
                   jax.ShapeDtypeStruct((batch * nc * W, LANES), BF16),
                   jax.ShapeDtypeStruct((batch * nc, SUBLANES, W), F32)],
        scratch_shapes=[pltpu.VMEM((seq, 3 * W), F32)],
        compiler_params=_params(("parallel",)),
        name="gdn_prepare",
    )(pg, pg, conv_w, a_row, dtb_row)
    group = GDN_GROUP if batch % GDN_GROUP == 0 else 1
    return pl.pallas_call(
        functools.partial(_gdn_recur_kernel, seq=seq, group=group),
        grid=(batch // group,),
        in_specs=[pl.BlockSpec((group * seq, W), lambda b: (b, 0)),
                  pl.BlockSpec((2 * group * seq, W), lambda b: (b, 0)),
                  pl.BlockSpec((group * seq, W), lambda b: (b, 0)),
                  pl.BlockSpec((group * nc * W, LANES), lambda b: (b, 0)),
                  pl.BlockSpec((group * nc, SUBLANES, W), lambda b: (b, 0, 0)),
                  pl.BlockSpec((group * seq, W), lambda b: (b, 3)),
                  _const_spec(norm_row.shape)],
        out_specs=pl.BlockSpec((group * seq, W), lambda b: (b, 0)),
        out_shape=jax.ShapeDtypeStruct((n, W), BF16),
        scratch_shapes=[pltpu.VMEM((group, W, W), F32)],
        compiler_params=_params(("parallel",)),
        name="gdn_recur",
    )(u, wq, aqk, kdt, gl, pg, norm_row)


def _ssd_kernel(xbc_ref, z_ref, dt_ref, convw_ref, convb_ref, alog_ref, dtb_ref, dskip_ref, norm_ref,
                o_ref, conv_ref, s_ref, *, seq):
    L = SSD_CHUNK
    W = BRANCH_WIDTH
    GN = SSD_GROUPS * SSD_STATE
    _conv_silu_into(xbc_ref, convw_ref, convb_ref[...], conv_ref, seq, L)
    s_ref[...] = jnp.zeros_like(s_ref)

    ri = lax.broadcasted_iota(jnp.int32, (L, L), 0)
    ci = lax.broadcasted_iota(jnp.int32, (L, L), 1)
    incl = ri >= ci
    tril = jnp.where(incl, 1.0, 0.0)
    first = lax.broadcasted_iota(jnp.int32, (1, LANES), 1) < HEAD_DIM
    neg_a = -jnp.exp(alog_ref[...])
    dtb = dtb_ref[...]
    dskip = dskip_ref[...]
    gain = norm_ref[...]

    groups = range(SSD_GROUPS)
    units = [(n, g) for n in range(SSD_UNROLL) for g in groups]
    pair_col = lambda t, g: jnp.where(first, t[:, PAIR * g:PAIR * g + 1], t[:, PAIR * g + 1:PAIR * g + 2])

    def body(i, carry):
        r0 = [pl.multiple_of((i * SSD_UNROLL + n) * L, L) for n in range(SSD_UNROLL)]
        xbc = [conv_ref[pl.ds(r, L), :] for r in r0]
        dt = [_softplus(dt_ref[pl.ds(r, L), :] + dtb) for r in r0]
        acs = [_hdot(tril, dt_ * neg_a) for dt_ in dt]
        acs_rows = [a_.T for a_ in acs]
        wide = lambda t, h: jnp.broadcast_to(t[:, h:h + 1], (L, LANES))
        acs_w = {(n, h): wide(acs[n], h) for n in range(SSD_UNROLL) for h in range(N_HEADS)}
        acs_p = {(n, g): jnp.where(first, acs_w[n, PAIR * g], acs_w[n, PAIR * g + 1]) for n, g in units}
        dt_p = {(n, g): jnp.where(first, wide(dt[n], PAIR * g), wide(dt[n], PAIR * g + 1)) for n, g in units}
        a_last = {u: acs_p[u][L - 1:L, :] for u in units}
        x2 = {(n, g): xbc[n][:, g * LANES:(g + 1) * LANES] for n, g in units}
        bm = {(n, g): xbc[n][:, W + g * SSD_STATE:W + (g + 1) * SSD_STATE] for n, g in units}
        cm = {(n, g): xbc[n][:, W + GN + g * SSD_STATE:W + GN + (g + 1) * SSD_STATE] for n, g in units}
        cb = {u: _bdot_nt(cm[u], bm[u]) for u in units}
        grow = {u: _bdot(bm[u].T, x2[u] * (dt_p[u] * jnp.exp(a_last[u] - acs_p[u]))) for u in units}
        s = {g: s_ref[g] for g in groups}
        y_off = {}
        for n, g in units:
            y_off[n, g] = _bdot(cm[n, g], s[g]) * jnp.exp(acs_p[n, g])
            s[g] = s[g] * jnp.exp(a_last[n, g]) + grow[n, g]
        for g in groups:
            s_ref[g] = s[g]
        xdt = {u: x2[u] * dt_p[u] for u in units}
        yd = {}
        for n, g in units:
            for h in (PAIR * g, PAIR * g + 1):
                lmat = jnp.exp(jnp.where(incl, acs_w[n, h] - acs_rows[n][h:h + 1, :], -jnp.inf))
                yd[n, h] = _bdot(cb[n, g] * lmat, xdt[n, g])
        for n in range(SSD_UNROLL):
            ys = [jnp.where(first, yd[n, PAIR * g], yd[n, PAIR * g + 1]) + y_off[n, g] for g in groups]
            y = jnp.concatenate(ys, axis=1) + dskip * xbc[n][:, :W]
            zt = z_ref[pl.ds(r0[n], L), :]
            o_ref[pl.ds(r0[n], L), :] = _rms(y * _silu(zt), gain).astype(o_ref.dtype)
        return carry

    lax.fori_loop(0, seq // (L * SSD_UNROLL), body, 0)


def _ssd(ps, conv_w, conv_b, a_row, dtb_row, dskip_row, norm_row, batch, seq):
    kern = functools.partial(_ssd_kernel, seq=seq)
    return pl.pallas_call(
        kern,
        grid=(batch,),
        in_specs=[pl.BlockSpec((seq, SSD_CONV_DIM), lambda b: (b, 0)),
                  pl.BlockSpec((seq, BRANCH_WIDTH), lambda b: (b, SSD_CONV_DIM // BRANCH_WIDTH)),
                  pl.BlockSpec((seq, SMALL_PAD), lambda b: (b, (SSD_CONV_DIM + BRANCH_WIDTH) // SMALL_PAD)),
                  _const_spec(conv_w.shape), _const_spec(conv_b.shape), _const_spec(a_row.shape),
                  _const_spec(dtb_row.shape), _const_spec(dskip_row.shape), _const_spec(norm_row.shape)],
        out_specs=pl.BlockSpec((seq, BRANCH_WIDTH), lambda b: (b, 0)),
        out_shape=jax.ShapeDtypeStruct((batch * seq, BRANCH_WIDTH), BF16),
        scratch_shapes=[pltpu.VMEM((seq, SSD_CONV_DIM), F32),
                        pltpu.VMEM((SSD_GROUPS, SSD_STATE, LANES), F32)],
        compiler_params=_params(("parallel",)),
        name="ssd",
    )(ps, ps, ps, conv_w, conv_b, a_row, dtb_row, dskip_row, norm_row)


def _head_masked(q, first_half):
    lane = lax.broadcasted_iota(jnp.int32, (1, LANES), 1)
    keep = (lane < HEAD_DIM) if first_half else (lane >= HEAD_DIM)
    return jnp.where(keep, q.astype(F32), 0.0)


def _moba_kernel(q_ref, k_ref, v_ref, o_ref, kmean_ref, vt_ref, *, seq):
    BS = MOBA_BLOCK
    NB = seq // BS
    W = BRANCH_WIDTH
    qb = pl.program_id(1)
    scale = HEAD_DIM ** -0.5
    q = q_ref[...]

    @pl.when(qb == 0)
    def _():
        kmean = jnp.sum(k_ref[...].astype(F32).reshape(NB, BS, W), axis=1) * (1.0 / BS)
        kmean_ref[...] = jnp.concatenate([kmean, jnp.zeros((SUBLANES, W), F32)], axis=0)
        for c in range(seq // LANES):
            blk = v_ref[c * LANES:(c + 1) * LANES, :].astype(F32)
            vt_ref[:, c * LANES:(c + 1) * LANES] = jnp.concatenate(
                [blk[:, :LANES].T, blk[:, LANES:].T], axis=0).astype(BF16)

    rel = (lax.broadcasted_iota(jnp.int32, (BS, BS), 1)
           - lax.broadcasted_iota(jnp.int32, (BS, BS), 0)).astype(F32)
    first_rows = lax.broadcasted_iota(jnp.int32, (LANES, 1), 0) < HEAD_DIM
    blk_id = lax.broadcasted_iota(jnp.int32, (SUBLANES, BS), 0)
    heads = range(N_HEADS)
    pair_of = [h // PAIR for h in heads]
    lanes_of = lambda x, p: x[:, p * LANES:(p + 1) * LANES]
    slopes = [2.0 ** (-8.0 * (h + 1) / N_HEADS) for h in heads]

    qf = [_head_masked(lanes_of(q, pair_of[h]), h % PAIR == 0) for h in heads]
    km = _split(kmean_ref[...], 3)
    gate = []
    for h in heads:
        qh = qf[h].astype(BF16)
        g_h = None
        for piece in km:
            t = lax.dot_general(lanes_of(piece, pair_of[h]), qh, (((1,), (1,)), ((), ())),
                                preferred_element_type=F32)
            g_h = t if g_h is None else g_h + t
        gate.append(g_h[:NB])
    sels = []
    for h in heads:
        rank = jnp.zeros((NB, BS), F32)
        for i in range(NB):
            gi = gate[h][i:i + 1, :]
            beats = jnp.where(gi > gate[h], 1.0, jnp.where((gi == gate[h]) & (blk_id > i), 1.0, 0.0))
            rank = rank + jnp.where(i < qb, beats, 0.0)
        sels.append(jnp.where((rank < MOBA_TOPK) & (blk_id < qb), 1.0, 0.0))
    qms = [(qf[h] * scale).astype(BF16) for h in heads]

    def blocks(js, carry, own):
        r0 = [pl.multiple_of(j * BS, BS) for j in js]
        k_j = [k_ref[pl.ds(r, BS), :] for r in r0]
        vt_j = [vt_ref[:, pl.ds(r, BS)] for r in r0]
        dist = [rel if own else rel + ((qb - j) * BS).astype(F32) for j in js]
        m, l, acc = carry[:N_HEADS], carry[N_HEADS:2 * N_HEADS], carry[2 * N_HEADS:]
        units = [(b, h) for b in range(len(js)) for h in heads]
        raw = {u: _bdot_nt(lanes_of(k_j[u[0]], pair_of[u[1]]), qms[u[1]]) for u in units}
        if own:
            keep = {u: rel >= 0 for u in units}
        else:
            keep = {(b, h): jnp.sum(jnp.where(blk_id == js[b], sels[h], 0.0), axis=0, keepdims=True) > 0.5
                    for b, h in units}
        s = {(b, h): jnp.where(keep[b, h], raw[b, h] - slopes[h] * dist[b], -jnp.inf) for b, h in units}
        m_new = list(m)
        for b, h in units:
            m_blk = jnp.max(s[b, h], axis=0, keepdims=True)
            m_new[h] = m_blk if m_new[h] is None else jnp.maximum(m_new[h], m_blk)
        p = {(b, h): jnp.exp(s[b, h] - m_new[h]) for b, h in units}
        pv, l_new = [None] * N_HEADS, [None] * N_HEADS
        for b, h in units:
            t = _bdot(vt_j[b][pair_of[h] * LANES:(pair_of[h] + 1) * LANES, :], p[b, h])
            pv[h] = t if pv[h] is None else pv[h] + t
            t = jnp.sum(p[b, h], axis=0, keepdims=True)
            l_new[h] = t if l_new[h] is None else l_new[h] + t
        if not own:
            alpha = [jnp.exp(m[h] - m_new[h]) for h in heads]
            l_new = [l[h] * alpha[h] + l_new[h] for h in heads]
            pv = [acc[h] * alpha[h] + pv[h] for h in heads]
        return tuple(m_new) + tuple(l_new) + tuple(pv)

    state = blocks([qb], (None,) * (3 * N_HEADS), True)
    out = lax.fori_loop(0, qb // 2, lambda i, c: blocks([2 * i, 2 * i + 1], c, False), state)
    out = lax.cond(qb % 2 == 1, lambda c: blocks([qb - 1], c, False), lambda c: c, out)
    l, acc = out[N_HEADS:2 * N_HEADS], out[2 * N_HEADS:]
    o = [acc[h] / l[h] for h in heads]
    pairs = [jnp.where(first_rows, o[PAIR * p_], o[PAIR * p_ + 1]) for p_ in range(N_HEADS // PAIR)]
    o_ref[...] = jnp.concatenate(
        [jnp.concatenate([t[:, :LANES].T, t[:, LANES:].T], axis=0) for t in pairs], axis=1).astype(o_ref.dtype)


def _moba(pa, batch, seq):
    BS = MOBA_BLOCK
    nq = seq // BS
    assert nq == SUBLANES
    W = BRANCH_WIDTH
    kern = functools.partial(_moba_kernel, seq=seq)
    return pl.pallas_call(
        kern,
        grid=(batch, nq),
        in_specs=[pl.BlockSpec((BS, W), lambda b, i: (b * nq + i, 0)),
                  pl.BlockSpec((seq, W), lambda b, i: (b, 1)),
                  pl.BlockSpec((seq, W), lambda b, i: (b, 2))],
        out_specs=pl.BlockSpec((BS, W), lambda b, i: (b * nq + i, 0)),
        out_shape=jax.ShapeDtypeStruct((batch * seq, W), BF16),
        scratch_shapes=[pltpu.VMEM((2 * SUBLANES, W), F32),
                        pltpu.VMEM((W, seq), BF16)],
        compiler_params=_params(("parallel", "arbitrary")),
        name="moba",
    )(pa, pa, pa)


def _sb_kernel(q_ref, k_ref, v_ref, o_ref, vt_ref, *, seq):
    TQ = SB_TILE
    qb = pl.program_id(1)
    scale = HEAD_DIM ** -0.5
    q = q_ref[...]

    @pl.when(qb == 0)
    def _():
        for c in range(seq // LANES):
            blk = v_ref[c * LANES:(c + 1) * LANES, :].astype(F32)
            vt_ref[:, c * LANES:(c + 1) * LANES] = jnp.concatenate(
                [blk[:, :LANES].T, blk[:, LANES:].T], axis=0).astype(BF16)

    before = lax.broadcasted_iota(jnp.int32, (TQ, TQ), 0) < lax.broadcasted_iota(jnp.int32, (TQ, TQ), 1)
    later = jnp.where(before, 1.0, 0.0).astype(BF16)
    first_rows = lax.broadcasted_iota(jnp.int32, (LANES, 1), 0) < HEAD_DIM
    heads = range(N_HEADS)
    pair_of = [h // PAIR for h in heads]
    lanes_of = lambda x, p: x[:, p * LANES:(p + 1) * LANES]
    qms = [(_head_masked(lanes_of(q, pair_of[h]), h % PAIR == 0) * scale).astype(BF16) for h in heads]

    def tile(kb, carry, diagonal):
        r0 = pl.multiple_of(kb * TQ, TQ)
        k_j = k_ref[pl.ds(r0, TQ), :]
        vt_j = vt_ref[:, pl.ds(r0, TQ)]
        tails, accs = carry[:N_HEADS], carry[N_HEADS:]
        z2 = [_bdot_nt(lanes_of(k_j, pair_of[h]), qms[h]) * LOG2E for h in heads]
        cost = [jnp.maximum(z_, 0.0) + jnp.log2(1.0 + jnp.exp2(-jnp.abs(z_))) for z_ in z2]
        if diagonal:
            cost = [jnp.where(before, c_, 0.0) for c_ in cost]
        after = [_dot_const_data(later, c_, 2) for c_ in cost]
        w = [jnp.exp2(z2[h] - cost[h] - after[h] - tails[h]) for h in heads]
        if diagonal:
            w = [jnp.where(before, w_, 0.0) for w_ in w]
        pv = [_bdot(vt_j[pair_of[h] * LANES:(pair_of[h] + 1) * LANES, :], w[h]) for h in heads]
        accs = [accs[h] + pv[h] for h in heads]
        tails = [tails[h] + jnp.sum(cost[h], axis=0, keepdims=True) for h in heads]
        return tuple(tails) + tuple(accs)

    def live(tails):
        least = tails[0]
        for t_ in tails[1:]:
            least = jnp.minimum(least, t_)
        return jnp.min(least) < SB_DEAD_TAIL

    init = tuple(jnp.zeros((1, TQ), F32) for _ in heads) + tuple(jnp.zeros((LANES, TQ), F32) for _ in heads)
    carry = tile(qb, init, True)

    def step(state):
        i = state[0]
        new = tile(qb - 1 - i, state[2:], False)
        return (i + 1, live(new[:N_HEADS])) + new

    out = lax.while_loop(lambda state: jnp.logical_and(state[0] < qb, state[1]), step,
                         (jnp.int32(0), live(carry[:N_HEADS])) + carry)
    accs = out[2 + N_HEADS:]
    pairs = [jnp.where(first_rows, accs[PAIR * p], accs[PAIR * p + 1]) for p in range(N_HEADS // PAIR)]
    o_ref[...] = jnp.concatenate(
        [jnp.concatenate([t[:, c * LANES:(c + 1) * LANES].T for c in range(TQ // LANES)], axis=0) for t in pairs],
        axis=1).astype(o_ref.dtype)


def _sb(pa, batch, seq):
    TQ = SB_TILE
    nq = seq // TQ
    W = BRANCH_WIDTH
    return pl.pallas_call(
        functools.partial(_sb_kernel, seq=seq),
        grid=(batch, nq),
        in_specs=[pl.BlockSpec((TQ, W), lambda b, i: (b * nq + i, 3)),
                  pl.BlockSpec((seq, W), lambda b, i: (b, 4)),
                  pl.BlockSpec((seq, W), lambda b, i: (b, 5))],
        out_specs=pl.BlockSpec((TQ, W), lambda b, i: (b * nq + i, 0)),
        out_shape=jax.ShapeDtypeStruct((batch * seq, W), BF16),
        scratch_shapes=[pltpu.VMEM((W, seq), BF16)],
        compiler_params=_params(("parallel", "arbitrary")),
        name="sb",
    )(pa, pa, pa)


def _merge_kernel(x_ref, ya_ref, yb_ref, yc_ref, yd_ref, gpre_ref, wgate_ref, wbr_ref, wout_ref, gpost_ref, o_ref):
    rows = x_ref.shape[0] // TOKEN_SPLIT
    for part in range(TOKEN_SPLIT):
        r = slice(part * rows, (part + 1) * rows)
        x = x_ref[r, :]
        h = _rms(x, gpre_ref[...]).astype(BF16)
        merged = None
        for g, y_ref in enumerate((ya_ref, yb_ref, yc_ref, yd_ref)):
            gate = _sigmoid(_dot(h, wgate_ref[g]))
            term = gate * _dot(y_ref[r, :], wbr_ref[g])
            merged = term if merged is None else merged + term
        mix = _dot(merged.astype(BF16), wout_ref[...])
        o_ref[r, :] = x + _rms(mix, gpost_ref[...])


def _merge(x, ya, yb, yc, yd, gpre, wgate, wbr, wout, gpost):
    n = x.shape[0]
    tm = TOKEN_TILE
    row = lambda w: pl.BlockSpec((tm, w), lambda i: (i, 0))
    return pl.pallas_call(
        _merge_kernel,
        grid=(n // tm,),
        in_specs=[row(D_MODEL)] + [row(BRANCH_WIDTH)] * N_BRANCH
        + [_const_spec(gpre.shape), _const_spec(wgate.shape), _const_spec(wbr.shape),
           _const_spec(wout.shape), _const_spec(gpost.shape)],
        out_specs=row(D_MODEL),
        out_shape=jax.ShapeDtypeStruct((n, D_MODEL), F32),
        compiler_params=_params(("parallel",)),
        name="merge",
    )(x, ya, yb, yc, yd, gpre, wgate, wbr, wout, gpost)


def _ffn_kernel(x_ref, gpre_ref, wup_ref, wdown_ref, gpost_ref, o_ref):
    rows = x_ref.shape[0] // TOKEN_SPLIT
    for part in range(TOKEN_SPLIT):
        r = slice(part * rows, (part + 1) * rows)
        x = x_ref[r, :]
        h = _rms(x, gpre_ref[...]).astype(BF16)
        f = None
        for c in range(D_FF // FF_CHUNK):
            u = _dot(h, wup_ref[:, c * FF_CHUNK:(c + 1) * FF_CHUNK])
            a = jnp.square(jnp.maximum(u, 0.0)).astype(BF16)
            t = _dot(a, wdown_ref[c * FF_CHUNK:(c + 1) * FF_CHUNK, :])
            f = t if f is None else f + t
        o_ref[r, :] = x + _rms(f, gpost_ref[...])


def _ffn(x, gpre, wup, wdown, gpost):
    n = x.shape[0]
    tm = TOKEN_TILE
    row = pl.BlockSpec((tm, D_MODEL), lambda i: (i, 0))
    return pl.pallas_call(
        _ffn_kernel,
        grid=(n // tm,),
        in_specs=[row, _const_spec(gpre.shape), _const_spec(wup.shape), _const_spec(wdown.shape),
                  _const_spec(gpost.shape)],
        out_specs=row,
        out_shape=jax.ShapeDtypeStruct((n, D_MODEL), F32),
        compiler_params=_params(("parallel",)),
        name="ffn",
    )(x, gpre, wup, wdown, gpost)


def _pad_lanes(a, width):
    return jnp.pad(a, ((0, 0), (0, width - a.shape[1])))


def _small_groups(w_in, layer):
    main = 4 * BRANCH_WIDTH
    gdn_in = main + 2 * N_HEADS
    wg_small = _pad_lanes(w_in[layer, :, main:gdn_in], SMALL_PAD)
    ws_small = _pad_lanes(w_in[layer, :, gdn_in + ATT_W + main:], SMALL_PAD)
    return wg_small.astype(BF16), ws_small.astype(BF16)


def kernel(x, norm_mix_pre, norm_mix_post, norm_ffn_pre, norm_ffn_post, w_in, gdn_conv, gdn_a_log, gdn_dt_bias,
           gdn_norm, ssd_conv, ssd_conv_bias, ssd_a_log, ssd_dt_bias, ssd_d, ssd_norm, w_gate, w_branch, w_out,
           w_up, w_down):
    batch, seq, d = x.shape
    assert d == D_MODEL and seq % MOBA_BLOCK == 0 and (batch * seq) % TOKEN_TILE == 0
    depth = w_in.shape[0]
    xf = x.reshape(batch * seq, d).astype(F32)
    row = lambda a: a.reshape(1, -1).astype(F32)
    for l in range(depth):
        wg_small, ws_small = _small_groups(w_in, l)
        pg, pa, ps = _in_proj(xf, row(norm_mix_pre[l]), w_in.astype(F32), l, wg_small, ws_small)
        gdn_a = _pad_lanes(jnp.concatenate([jnp.zeros((N_HEADS,), F32), gdn_a_log[l]]).reshape(1, -1), SMALL_PAD)
        gdn_b = _pad_lanes(jnp.concatenate([jnp.zeros((N_HEADS,), F32), gdn_dt_bias[l]]).reshape(1, -1), SMALL_PAD)
        ya = _gdn(pg, gdn_conv[l].astype(F32), gdn_a, gdn_b, row(jnp.tile(gdn_norm[l], N_HEADS)), batch, seq)
        yb = _moba(pa, batch, seq)
        yc = _sb(pa, batch, seq)
        yd = _ssd(ps, ssd_conv[l].astype(F32), row(ssd_conv_bias[l]),
                  _pad_lanes(row(ssd_a_log[l]), SMALL_PAD), _pad_lanes(row(ssd_dt_bias[l]), SMALL_PAD),
                  row(jnp.repeat(ssd_d[l], HEAD_DIM)), row(ssd_norm[l]), batch, seq)
        xf = _merge(xf, ya, yb, yc, yd, row(norm_mix_pre[l]), w_gate[l].astype(BF16), w_branch[l].astype(BF16),
                    w_out[l].astype(BF16), row(norm_mix_post[l]))
        xf = _ffn(xf, row(norm_ffn_pre[l]), w_up[l].astype(BF16), w_down[l].astype(BF16), row(norm_ffn_post[l]))
    return xf.reshape(batch, seq, d).astype(x.dtype)
```

```python
import functools

import jax
import jax.numpy as jnp
from jax import lax
from jax.experimental import pallas as pl
from jax.experimental.pallas import tpu as pltpu

F32 = jnp.float32
BF16 = jnp.bfloat16
HIGHEST = lax.Precision.HIGHEST

D_MODEL = 1024
N_BRANCH = 4
BRANCH_WIDTH = 256
HEAD_DIM = 64
N_HEADS = 4
CONV_WIDTH = 4
GDN_CHUNK = 64
MOBA_BLOCK = 256
MOBA_TOPK = 3
SSD_STATE = 128
SSD_GROUPS = 2
SSD_CHUNK = 128
D_FF = 4 * D_MODEL
EPS = 1e-6

LANES = 128
SUBLANES = 8
PAIR = LANES // HEAD_DIM
SMALL_PAD = LANES
CONV_HALO = 8
TOKEN_TILE = 512
TOKEN_SPLIT = 2
FF_CHUNK = 1024
SB_TILE = 256
LOG2E = 1.4426950408889634
SB_DEAD_TAIL = 256.0
SSD_UNROLL = 4
GDN_GROUP = 2
GDN_UNROLL = 8
VMEM_LIMIT = 56 * 1024 * 1024

GDN_W = 3 * BRANCH_WIDTH + BRANCH_WIDTH + SMALL_PAD
ATT_W = 6 * BRANCH_WIDTH
SSD_CONV_DIM = BRANCH_WIDTH + 2 * SSD_GROUPS * SSD_STATE
SSD_W = SSD_CONV_DIM + BRANCH_WIDTH + SMALL_PAD


def _dot(a, b):
    return jnp.dot(a, b, preferred_element_type=F32)


def _bdot(a, b):
    return _dot(a.astype(BF16), b.astype(BF16))


def _bdot_nt(a, b):
    return lax.dot_general(a.astype(BF16), b.astype(BF16), (((1,), (1,)), ((), ())),
                           preferred_element_type=F32)


def _hdot(a, b):
    return jnp.dot(a, b, preferred_element_type=F32, precision=HIGHEST)


def _split(x, pieces):
    out = []
    for _ in range(pieces - 1):
        p = x.astype(BF16)
        out.append(p)
        x = x - p.astype(F32)
    out.append(x.astype(BF16))
    return out


def _dot_data_const(x, m, pieces):
    acc = None
    for p in _split(x, pieces):
        t = _dot(p, m)
        acc = t if acc is None else acc + t
    return acc


def _dot_const_data(m, x, pieces):
    acc = None
    for p in _split(x, pieces):
        t = _dot(m, p)
        acc = t if acc is None else acc + t
    return acc


def _sigmoid(x):
    return 1.0 / (1.0 + jnp.exp(-x))


def _silu(x):
    return x * _sigmoid(x)


def _softplus(x):
    return jnp.maximum(x, 0.0) + jnp.log(1.0 + jnp.exp(-jnp.abs(x)))


def _rms(x, gain):
    return x * lax.rsqrt(jnp.mean(x * x, axis=-1, keepdims=True) + EPS) * gain


def _params(sem):
    return pltpu.CompilerParams(dimension_semantics=sem, vmem_limit_bytes=VMEM_LIMIT)


def _const_spec(shape):
    nd = len(shape)
    return pl.BlockSpec(shape, lambda *_: (0,) * nd, pipeline_mode=pl.Buffered(1))


def _cols(w_ref, start, width):
    base = start // LANES * LANES
    stop = min(-(-(start + width) // LANES) * LANES, w_ref.shape[1])
    return w_ref[:, base:stop][:, start - base:start - base + width].astype(BF16)


def _in_proj_kernel(x_ref, gain_ref, w_ref, wgs_ref, wss_ref, og_ref, oa_ref, os_ref, wg_ref, wa_ref, ws_ref):
    W = BRANCH_WIDTH
    main = 4 * W

    @pl.when(pl.program_id(0) == 0)
    def _():
        off_att = main + 2 * N_HEADS
        off_ssd = off_att + ATT_W
        for c in range(main // W):
            wg_ref[:, c * W:(c + 1) * W] = _cols(w_ref, c * W, W)
        for c in range(ATT_W // W):
            wa_ref[:, c * W:(c + 1) * W] = _cols(w_ref, off_att + c * W, W)
        for c in range(SSD_CONV_DIM // W):
            ws_ref[:, c * W:(c + 1) * W] = _cols(w_ref, off_ssd + W + c * W, W)
        ws_ref[:, SSD_CONV_DIM:main] = _cols(w_ref, off_ssd, W)

    rows = x_ref.shape[0] // TOKEN_SPLIT
    for part in range(TOKEN_SPLIT):
        r = slice(part * rows, (part + 1) * rows)
        h = _rms(x_ref[r, :], gain_ref[...]).astype(BF16)
        og_ref[r, :main] = _dot(h, wg_ref[...])
        og_ref[r, main:] = _dot(h, wgs_ref[...])
        oa_ref[r, :] = _dot(h, wa_ref[...]).astype(BF16)
        os_ref[r, :main] = _dot(h, ws_ref[...])
        os_ref[r, main:] = _dot(h, wss_ref[...])


def _in_proj(x, gain, w_in, layer, wg_small, ws_small):
    n = x.shape[0]
    tm = TOKEN_TILE
    row = lambda w: pl.BlockSpec((tm, w), lambda i: (i, 0))
    main = 4 * BRANCH_WIDTH
    return pl.pallas_call(
        _in_proj_kernel,
        grid=(n // tm,),
        in_specs=[row(D_MODEL), _const_spec((1, D_MODEL)),
                  pl.BlockSpec((None,) + w_in.shape[1:], lambda i: (layer, 0, 0), pipeline_mode=pl.Buffered(1)),
                  _const_spec(wg_small.shape), _const_spec(ws_small.shape)],
        out_specs=[row(GDN_W), row(ATT_W), row(SSD_W)],
        out_shape=[jax.ShapeDtypeStruct((n, GDN_W), F32), jax.ShapeDtypeStruct((n, ATT_W), BF16),
                   jax.ShapeDtypeStruct((n, SSD_W), F32)],
        scratch_shapes=[pltpu.VMEM((D_MODEL, main), BF16), pltpu.VMEM((D_MODEL, ATT_W), BF16),
                        pltpu.VMEM((D_MODEL, main), BF16)],
        compiler_params=_params(("arbitrary",)),
        name="in_proj",
    )(x, gain, w_in, wg_small, ws_small)


def _conv_silu_into(src_ref, w_ref, bias, dst_ref, seq, chunk):
    w = w_ref[...]
    taps = [w[k:k + 1, :] for k in range(CONV_WIDTH)]

    def finish(acc):
        if bias is not None:
            acc = acc + bias
        return _silu(acc)

    blk = src_ref[0:chunk, :]
    rows = lax.broadcasted_iota(jnp.int32, (chunk, 1), 0)
    acc = taps[CONV_WIDTH - 1] * blk
    for k in range(CONV_WIDTH - 1):
        shift = CONV_WIDTH - 1 - k
        acc = acc + taps[k] * jnp.where(rows >= shift, pltpu.roll(blk, shift, 0), 0.0)
    dst_ref[0:chunk, :] = finish(acc)

    def body(c, carry):
        r0 = pl.multiple_of(c * chunk, chunk)
        blk = src_ref[pl.ds(r0 - CONV_HALO, chunk + CONV_HALO), :]
        acc = taps[CONV_WIDTH - 1] * blk[CONV_HALO:]
        for k in range(CONV_WIDTH - 1):
            acc = acc + taps[k] * pltpu.roll(blk, CONV_WIDTH - 1 - k, 0)[CONV_HALO:]
        dst_ref[pl.ds(r0, chunk), :] = finish(acc)
        return carry

    lax.fori_loop(1, seq // chunk, body, 0)


def _block_diag(x, mask):
    return jnp.where(mask, jnp.concatenate([x] * N_HEADS, axis=0), jnp.zeros((), x.dtype))


def _dot1_bd(a, b, mask):
    return _dot(a.astype(BF16), _block_diag(b.astype(BF16), mask))


def _dot3_bd(a, b, mask):
    ah, al = _split(a, 2)
    bh, bl = _split(b, 2)
    bdh = _block_diag(bh, mask)
    return _dot(ah, bdh) + _dot(al, bdh) + _dot(ah, _block_diag(bl, mask))


def _gdn_prepare_kernel(qkv_ref, bg_ref, convw_ref, alog_ref, dtb_ref,
                        u_ref, wq_ref, aqk_ref, kdt_ref, gl_ref, conv_ref, *, seq):
    C = GDN_CHUNK
    W = BRANCH_WIDTH
    NC = seq // C
    _conv_silu_into(qkv_ref, convw_ref, None, conv_ref, seq, 4 * C)

    def iota(shape, dim):
        return lax.broadcasted_iota(jnp.int32, shape, dim)

    same_head = (iota((W, W), 0) // HEAD_DIM) == (iota((W, W), 1) // HEAD_DIM)
    same_head2 = jnp.concatenate([same_head, same_head], axis=1)
    seg_ones = jnp.where(same_head, 1.0, 0.0).astype(BF16)
    r_c = iota((C, W), 0)
    j_c = iota((C, W), 1) % HEAD_DIM
    incl = r_c >= j_c
    strict = r_c > j_c
    eye = jnp.where(r_c == j_c, 1.0, 0.0)
    blocks = [(r_c // n) == (j_c // n) for n in (SUBLANES, 2 * SUBLANES, 4 * SUBLANES, C)]
    tril = jnp.where(iota((C, C), 0) >= iota((C, C), 1), 1.0, 0.0).astype(BF16)
    lane = iota((1, LANES), 1)
    ex_r = iota((LANES, 2 * W), 0)
    ex_c = iota((LANES, 2 * W), 1)
    expand = jnp.where(ex_r == jnp.where(ex_c < W, ex_c // HEAD_DIM, N_HEADS + (ex_c - W) // HEAD_DIM),
                       1.0, 0.0).astype(BF16)
    neg_a = -jnp.exp(alog_ref[...])
    dtb = dtb_ref[...]
    scale = HEAD_DIM ** -0.5

    def row_form(gc):
        t = jnp.concatenate([gc, jnp.zeros((LANES - C, LANES), F32)], axis=0).T
        heads_rows = [t[N_HEADS + h:N_HEADS + h + 1, :] for h in range(N_HEADS)]
        tiles = [jnp.where(lane < HEAD_DIM, heads_rows[PAIR * p_], pltpu.roll(heads_rows[PAIR * p_ + 1], HEAD_DIM, 1))
                 for p_ in range(N_HEADS // PAIR)]
        return jnp.broadcast_to(jnp.concatenate(tiles, axis=1), (C, W))

    def prepare(i, carry):
        cs = [i * GDN_UNROLL + n for n in range(GDN_UNROLL)]
        r0s = [pl.multiple_of(c * C, C) for c in cs]
        each = lambda f, *xs: [f(*a) for a in zip(*xs)]
        qkv = [conv_ref[pl.ds(r0, C), :] for r0 in r0s]
        q = [a[:, :W] for a in qkv]
        k = [a[:, W:2 * W] for a in qkv]
        v = [a[:, 2 * W:] for a in qkv]
        ss = each(lambda q_, k_: _dot_data_const(jnp.concatenate([q_ * q_, k_ * k_], axis=0), seg_ones, 2), q, k)
        qn = each(lambda q_, s_: q_ * lax.rsqrt(s_[:C] + EPS) * scale, q, ss)
        kn = each(lambda k_, s_: k_ * lax.rsqrt(s_[C:] + EPS), k, ss)
        bg = [bg_ref[pl.ds(r0, C), :] for r0 in r0s]
        gc = each(lambda b_: _dot_const_data(tril, neg_a * _softplus(b_ + dtb), 3), bg)
        ex = each(lambda b_, g_: _dot_data_const(jnp.where(lane < N_HEADS, _sigmoid(b_), g_), expand, 3), bg, gc)
        beta = [a[:, :W] for a in ex]
        gcc = [a[:, W:] for a in ex]
        gcr = each(row_form, gc)
        dm = each(lambda a_, b_: jnp.exp(jnp.where(incl, a_ - b_, -jnp.inf)), gcc, gcr)
        kb = each(lambda a_, b_: a_ * b_, kn, beta)
        vb = each(lambda a_, b_: a_ * b_, v, beta)
        qk = each(lambda q_, kb_, kn_: lax.dot_general(
            jnp.concatenate([q_, kb_], axis=0).astype(BF16), _block_diag(kn_.astype(BF16), same_head),
            (((1,), (1,)), ((), ())), preferred_element_type=F32), qn, kb, kn)
        a_qk = each(lambda x_, d_: x_[:C] * d_, qk, dm)
        a = each(lambda x_, d_: jnp.where(strict, x_[C:] * d_, 0.0), qk, dm)
        d = [jnp.where(blocks[0], a_, 0.0) for a_ in a]
        p = [eye - d_ for d_ in d]
        b = each(lambda d_: _dot3_bd(d_, d_, same_head), d)
        pb = each(lambda p_, b_: _dot3_bd(jnp.concatenate([p_, b_], axis=0), b_, same_head), p, b)
        p = each(lambda p_, x_: p_ + x_[:C], p, pb)
        p = each(lambda p_, x_: p_ + _dot3_bd(p_, x_[C:], same_head), p, pb)
        for inner, outer in zip(blocks[:-1], blocks[1:]):
            pe = each(lambda p_, a_: _dot1_bd(p_, jnp.where(outer & ~inner, a_, 0.0), same_head), p, a)
            p = each(lambda p_, x_: p_ - _dot1_bd(x_, p_, same_head), p, pe)
        t_inv = p
        e_gc = [jnp.exp(g_) for g_ in gcc]
        uw = each(lambda t_, vb_, kb_, e_: _dot3_bd(t_, jnp.concatenate([vb_, kb_ * e_], axis=1), same_head2),
                  t_inv, vb, kb, e_gc)
        for n, c in enumerate(cs):
            r0 = r0s[n]
            g_last = gcc[n][C - 1:C, :]
            u_ref[pl.ds(r0, C), :] = uw[n][:, :W]
            wq_ref[pl.ds(pl.multiple_of(c * 2 * C, 2 * C), 2 * C), :] = jnp.concatenate(
                [uw[n][:, W:], qn[n] * e_gc[n]], axis=0).astype(BF16)
            aqk_ref[pl.ds(r0, C), :] = a_qk[n].astype(BF16)
            k_dec = jnp.concatenate([kn[n] * jnp.exp(g_last - gcc[n]), jnp.zeros((LANES - C, W), F32)], axis=0)
            kdt_ref[pl.ds(pl.multiple_of(c * W, W), W), :] = jnp.concatenate(
                [k_dec[:, :LANES].T, k_dec[:, LANES:].T], axis=0).astype(BF16)
            gl_ref[c] = jnp.broadcast_to(jnp.exp(g_last), (SUBLANES, W))
        return carry

    lax.fori_loop(0, NC // GDN_UNROLL, prepare, 0)


def _gdn_recur_kernel(u_ref, wq_ref, aqk_ref, kdt_ref, gl_ref, z_ref, norm_ref, o_ref, s_ref, *, seq, group):
    C = GDN_CHUNK
    W = BRANCH_WIDTH
    NC = seq // C
    same_head = ((lax.broadcasted_iota(jnp.int32, (W, W), 0) // HEAD_DIM)
                 == (lax.broadcasted_iota(jnp.int32, (W, W), 1) // HEAD_DIM))
    seg_ones = jnp.where(same_head, 1.0, 0.0).astype(BF16)
    gain = norm_ref[...]
    seqs = range(group)
    s_ref[...] = jnp.zeros_like(s_ref)

    def recur(c, carry):
        rows = [pl.multiple_of(g * seq + c * C, C) for g in seqs]
        rows2 = [pl.multiple_of(2 * (g * seq + c * C), 2 * C) for g in seqs]
        rows_k = [pl.multiple_of((g * NC + c) * W, W) for g in seqs]
        s = [s_ref[g] for g in seqs]
        ws_qs = [_dot(wq_ref[pl.ds(rows2[g], 2 * C), :], s[g].astype(BF16)) for g in seqs]
        v_new = [(u_ref[pl.ds(rows[g], C), :] - ws_qs[g][:C]).astype(BF16) for g in seqs]
        upd = [_dot(kdt_ref[pl.ds(rows_k[g], W), :],
                    jnp.concatenate([v_new[g], jnp.zeros((LANES - C, W), BF16)], axis=0)) for g in seqs]
        for g in seqs:
            s_ref[g] = s[g] * gl_ref[g * NC + c][0:1, :] + jnp.where(same_head, upd[g], 0.0)
        o = [ws_qs[g][C:] + _dot(aqk_ref[pl.ds(rows[g], C), :], _block_diag(v_new[g], same_head)) for g in seqs]
        ms = [_dot_data_const(o_ * o_, seg_ones, 2) * (1.0 / HEAD_DIM) for o_ in o]
        for g in seqs:
            y = o[g] * lax.rsqrt(ms[g] + EPS) * gain * _silu(z_ref[pl.ds(rows[g], C), :])
            o_ref[pl.ds(rows[g], C), :] = y.astype(o_ref.dtype)
        return carry

    lax.fori_loop(0, NC, recur, 0)


def _gdn(pg, conv_w, a_row, dtb_row, norm_row, batch, seq):
    C = GDN_CHUNK
    W = BRANCH_WIDTH
    nc = seq // C
    n = batch * seq
    u, wq, aqk, kdt, gl = pl.pallas_call(
        functools.partial(_gdn_prepare_kernel, seq=seq),
        grid=(batch,),
        in_specs=[pl.BlockSpec((seq, 3 * W), lambda b: (b, 0)),
                  pl.BlockSpec((seq, SMALL_PAD), lambda b: (b, 4 * W // SMALL_PAD)),
                  _const_spec(conv_w.shape), _const_spec(a_row.shape), _const_spec(dtb_row.shape)],
        out_specs=[pl.BlockSpec((seq, W), lambda b: (b, 0)),
                   pl.BlockSpec((2 * seq, W), lambda b: (b, 0)),
                   pl.BlockSpec((seq, W), lambda b: (b, 0)),
                   pl.BlockSpec((nc * W, LANES), lambda b: (b, 0)),
                   pl.BlockSpec((nc, SUBLANES, W), lambda b: (b, 0, 0))],
        out_shape=[jax.ShapeDtypeStruct((n, W), F32),
                   jax.ShapeDtypeStruct((2 * n, W), BF16),
                   jax.ShapeDtypeStruct((n, W), BF16),
                   jax.ShapeDtypeStruct((batch * nc * W, LANES), BF16),
                   jax.ShapeDtypeStruct((batch * nc, SUBLANES, W), F32)],
        scratch_shapes=[pltpu.VMEM((seq, 3 * W), F32)],
        compiler_params=_params(("parallel",)),
        name="gdn_prepare",
    )(pg, pg, conv_w, a_row, dtb_row)
    group = GDN_GROUP if batch % GDN_GROUP == 0 else 1
    return pl.pallas_call(
        functools.partial(_gdn_recur_kernel, seq=seq, group=group),
        grid=(batch // group,),
        in_specs=[pl.BlockSpec((group * seq, W), lambda b: (b, 0)),
                  pl.BlockSpec((2 * group * seq, W), lambda b: (b, 0)),
                  pl.BlockSpec((group * seq, W), lambda b: (b, 0)),
                  pl.BlockSpec((group * nc * W, LANES), lambda b: (b, 0)),
                  pl.BlockSpec((group * nc, SUBLANES, W), lambda b: (b, 0, 0)),
                  pl.BlockSpec((group * seq, W), lambda b: (b, 3)),
                  _const_spec(norm_row.shape)],
        out_specs=pl.BlockSpec((group * seq, W), lambda b: (b, 0)),
        out_shape=jax.ShapeDtypeStruct((n, W), BF16),
        scratch_shapes=[pltpu.VMEM((group, W, W), F32)],
        compiler_params=_params(("parallel",)),
        name="gdn_recur",
    )(u, wq, aqk, kdt, gl, pg, norm_row)


def _ssd_kernel(xbc_ref, z_ref, dt_ref, convw_ref, convb_ref, alog_ref, dtb_ref, dskip_ref, norm_ref,
                o_ref, conv_ref, s_ref, *, seq):
    L = SSD_CHUNK
    W = BRANCH_WIDTH
    GN = SSD_GROUPS * SSD_STATE
    _conv_silu_into(xbc_ref, convw_ref, convb_ref[...], conv_ref, seq, L)
    s_ref[...] = jnp.zeros_like(s_ref)

    ri = lax.broadcasted_iota(jnp.int32, (L, L), 0)
    ci = lax.broadcasted_iota(jnp.int32, (L, L), 1)
    incl = ri >= ci
    tril = jnp.where(incl, 1.0, 0.0)
    first = lax.broadcasted_iota(jnp.int32, (1, LANES), 1) < HEAD_DIM
    neg_a = -jnp.exp(alog_ref[...])
    dtb = dtb_ref[...]
    dskip = dskip_ref[...]
    gain = norm_ref[...]

    groups = range(SSD_GROUPS)
    units = [(n, g) for n in range(SSD_UNROLL) for g in groups]
    pair_col = lambda t, g: jnp.where(first, t[:, PAIR * g:PAIR * g + 1], t[:, PAIR * g + 1:PAIR * g + 2])

    def body(i, carry):
        r0 = [pl.multiple_of((i * SSD_UNROLL + n) * L, L) for n in range(SSD_UNROLL)]
        xbc = [conv_ref[pl.ds(r, L), :] for r in r0]
        dt = [_softplus(dt_ref[pl.ds(r, L), :] + dtb) for r in r0]
        acs = [_hdot(tril, dt_ * neg_a) for dt_ in dt]
        acs_rows = [a_.T for a_ in acs]
        wide = lambda t, h: jnp.broadcast_to(t[:, h:h + 1], (L, LANES))
        acs_w = {(n, h): wide(acs[n], h) for n in range(SSD_UNROLL) for h in range(N_HEADS)}
        acs_p = {(n, g): jnp.where(first, acs_w[n, PAIR * g], acs_w[n, PAIR * g + 1]) for n, g in units}
        dt_p = {(n, g): jnp.where(first, wide(dt[n], PAIR * g), wide(dt[n], PAIR * g + 1)) for n, g in units}
        a_last = {u: acs_p[u][L - 1:L, :] for u in units}
        x2 = {(n, g): xbc[n][:, g * LANES:(g + 1) * LANES] for n, g in units}
        bm = {(n, g): xbc[n][:, W + g * SSD_STATE:W + (g + 1) * SSD_STATE] for n, g in units}
        cm = {(n, g): xbc[n][:, W + GN + g * SSD_STATE:W + GN + (g + 1) * SSD_STATE] for n, g in units}
        cb = {u: _bdot_nt(cm[u], bm[u]) for u in units}
        grow = {u: _bdot(bm[u].T, x2[u] * (dt_p[u] * jnp.exp(a_last[u] - acs_p[u]))) for u in units}
        s = {g: s_ref[g] for g in groups}
        y_off = {}
        for n, g in units:
            y_off[n, g] = _bdot(cm[n, g], s[g]) * jnp.exp(acs_p[n, g])
            s[g] = s[g] * jnp.exp(a_last[n, g]) + grow[n, g]
        for g in groups:
            s_ref[g] = s[g]
        xdt = {u: x2[u] * dt_p[u] for u in units}
        yd = {}
        for n, g in units:
            for h in (PAIR * g, PAIR * g + 1):
                lmat = jnp.exp(jnp.where(incl, acs_w[n, h] - acs_rows[n][h:h + 1, :], -jnp.inf))
                yd[n, h] = _bdot(cb[n, g] * lmat, xdt[n, g])
        for n in range(SSD_UNROLL):
            ys = [jnp.where(first, yd[n, PAIR * g], yd[n, PAIR * g + 1]) + y_off[n, g] for g in groups]
            y = jnp.concatenate(ys, axis=1) + dskip * xbc[n][:, :W]
            zt = z_ref[pl.ds(r0[n], L), :]
            o_ref[pl.ds(r0[n], L), :] = _rms(y * _silu(zt), gain).astype(o_ref.dtype)
        return carry

    lax.fori_loop(0, seq // (L * SSD_UNROLL), body, 0)


def _ssd(ps, conv_w, conv_b, a_row, dtb_row, dskip_row, norm_row, batch, seq):
    kern = functools.partial(_ssd_kernel, seq=seq)
    return pl.pallas_call(
        kern,
        grid=(batch,),
        in_specs=[pl.BlockSpec((seq, SSD_CONV_DIM), lambda b: (b, 0)),
                  pl.BlockSpec((seq, BRANCH_WIDTH), lambda b: (b, SSD_CONV_DIM // BRANCH_WIDTH)),
                  pl.BlockSpec((seq, SMALL_PAD), lambda b: (b, (SSD_CONV_DIM + BRANCH_WIDTH) // SMALL_PAD)),
                  _const_spec(conv_w.shape), _const_spec(conv_b.shape), _const_spec(a_row.shape),
                  _const_spec(dtb_row.shape), _const_spec(dskip_row.shape), _const_spec(norm_row.shape)],
        out_specs=pl.BlockSpec((seq, BRANCH_WIDTH), lambda b: (b, 0)),
        out_shape=jax.ShapeDtypeStruct((batch * seq, BRANCH_WIDTH), BF16),
        scratch_shapes=[pltpu.VMEM((seq, SSD_CONV_DIM), F32),
                        pltpu.VMEM((SSD_GROUPS, SSD_STATE, LANES), F32)],
        compiler_params=_params(("parallel",)),
        name="ssd",
    )(ps, ps, ps, conv_w, conv_b, a_row, dtb_row, dskip_row, norm_row)


def _head_masked(q, first_half):
    lane = lax.broadcasted_iota(jnp.int32, (1, LANES), 1)
    keep = (lane < HEAD_DIM) if first_half else (lane >= HEAD_DIM)
    return jnp.where(keep, q.astype(F32), 0.0)


def _moba_kernel(q_ref, k_ref, v_ref, o_ref, kmean_ref, vt_ref, *, seq):
    BS = MOBA_BLOCK
    NB = seq // BS
    W = BRANCH_WIDTH
    qb = pl.program_id(1)
    scale = HEAD_DIM ** -0.5
    q = q_ref[...]

    @pl.when(qb == 0)
    def _():
        kmean = jnp.sum(k_ref[...].astype(F32).reshape(NB, BS, W), axis=1) * (1.0 / BS)
        kmean_ref[...] = jnp.concatenate([kmean, jnp.zeros((SUBLANES, W), F32)], axis=0)
        for c in range(seq // LANES):
            blk = v_ref[c * LANES:(c + 1) * LANES, :].astype(F32)
            vt_ref[:, c * LANES:(c + 1) * LANES] = jnp.concatenate(
                [blk[:, :LANES].T, blk[:, LANES:].T], axis=0).astype(BF16)

    rel = (lax.broadcasted_iota(jnp.int32, (BS, BS), 1)
           - lax.broadcasted_iota(jnp.int32, (BS, BS), 0)).astype(F32)
    first_rows = lax.broadcasted_iota(jnp.int32, (LANES, 1), 0) < HEAD_DIM
    blk_id = lax.broadcasted_iota(jnp.int32, (SUBLANES, BS), 0)
    heads = range(N_HEADS)
    pair_of = [h // PAIR for h in heads]
    lanes_of = lambda x, p: x[:, p * LANES:(p + 1) * LANES]
    slopes = [2.0 ** (-8.0 * (h + 1) / N_HEADS) for h in heads]

    qf = [_head_masked(lanes_of(q, pair_of[h]), h % PAIR == 0) for h in heads]
    km = _split(kmean_ref[...], 3)
    gate = []
    for h in heads:
        qh = qf[h].astype(BF16)
        g_h = None
        for piece in km:
            t = lax.dot_general(lanes_of(piece, pair_of[h]), qh, (((1,), (1,)), ((), ())),
                                preferred_element_type=F32)
            g_h = t if g_h is None else g_h + t
        gate.append(g_h[:NB])
    sels = []
    for h in heads:
        rank = jnp.zeros((NB, BS), F32)
        for i in range(NB):
            gi = gate[h][i:i + 1, :]
            beats = jnp.where(gi > gate[h], 1.0, jnp.where((gi == gate[h]) & (blk_id > i), 1.0, 0.0))
            rank = rank + jnp.where(i < qb, beats, 0.0)
        sels.append(jnp.where((rank < MOBA_TOPK) & (blk_id < qb), 1.0, 0.0))
    qms = [(qf[h] * scale).astype(BF16) for h in heads]

    def blocks(js, carry, own_first):
        r0 = [pl.multiple_of(j * BS, BS) for j in js]
        k_j = [k_ref[pl.ds(r, BS), :] for r in r0]
        vt_j = [vt_ref[:, pl.ds(r, BS)] for r in r0]
        is_own = lambda b: own_first and b == 0
        dist = [rel if is_own(b) else rel + ((qb - j) * BS).astype(F32) for b, j in enumerate(js)]
        m, l, acc = carry[:N_HEADS], carry[N_HEADS:2 * N_HEADS], carry[2 * N_HEADS:]
        units = [(b, h) for b in range(len(js)) for h in heads]
        raw = {u: _bdot_nt(lanes_of(k_j[u[0]], pair_of[u[1]]), qms[u[1]]) for u in units}
        keep = {(b, h): (rel >= 0) if is_own(b) else
                jnp.sum(jnp.where(blk_id == js[b], sels[h], 0.0), axis=0, keepdims=True) > 0.5 for b, h in units}
        s = {(b, h): jnp.where(keep[b, h], raw[b, h] - slopes[h] * dist[b], -jnp.inf) for b, h in units}
        m_new = list(m)
        for b, h in units:
            m_blk = jnp.max(s[b, h], axis=0, keepdims=True)
            m_new[h] = m_blk if m_new[h] is None else jnp.maximum(m_new[h], m_blk)
        p = {(b, h): jnp.exp(s[b, h] - m_new[h]) for b, h in units}
        pv, l_new = [None] * N_HEADS, [None] * N_HEADS
        for b, h in units:
            t = _bdot(vt_j[b][pair_of[h] * LANES:(pair_of[h] + 1) * LANES, :], p[b, h])
            pv[h] = t if pv[h] is None else pv[h] + t
            t = jnp.sum(p[b, h], axis=0, keepdims=True)
            l_new[h] = t if l_new[h] is None else l_new[h] + t
        if not own_first:
            alpha = [jnp.exp(m[h] - m_new[h]) for h in heads]
            l_new = [l[h] * alpha[h] + l_new[h] for h in heads]
            pv = [acc[h] * alpha[h] + pv[h] for h in heads]
        return tuple(m_new) + tuple(l_new) + tuple(pv)

    none = (None,) * (3 * N_HEADS)
    state = lax.cond(qb >= 2, lambda: blocks([qb, 0, 1], none, True), lambda: blocks([qb], none, True))
    out = lax.fori_loop(1, qb // 2, lambda i, c: blocks([2 * i, 2 * i + 1], c, False), state)
    out = lax.cond(qb % 2 == 1, lambda c: blocks([qb - 1], c, False), lambda c: c, out)
    l, acc = out[N_HEADS:2 * N_HEADS], out[2 * N_HEADS:]
    o = [acc[h] / l[h] for h in heads]
    pairs = [jnp.where(first_rows, o[PAIR * p_], o[PAIR * p_ + 1]) for p_ in range(N_HEADS // PAIR)]
    o_ref[...] = jnp.concatenate(
        [jnp.concatenate([t[:, :LANES].T, t[:, LANES:].T], axis=0) for t in pairs], axis=1).astype(o_ref.dtype)


def _moba(pa, batch, seq):
    BS = MOBA_BLOCK
    nq = seq // BS
    assert nq == SUBLANES
    W = BRANCH_WIDTH
    kern = functools.partial(_moba_kernel, seq=seq)
    return pl.pallas_call(
        kern,
        grid=(batch, nq),
        in_specs=[pl.BlockSpec((BS, W), lambda b, i: (b * nq + i, 0)),
                  pl.BlockSpec((seq, W), lambda b, i: (b, 1)),
                  pl.BlockSpec((seq, W), lambda b, i: (b, 2))],
        out_specs=pl.BlockSpec((BS, W), lambda b, i: (b * nq + i, 0)),
        out_shape=jax.ShapeDtypeStruct((batch * seq, W), BF16),
        scratch_shapes=[pltpu.VMEM((2 * SUBLANES, W), F32),
                        pltpu.VMEM((W, seq), BF16)],
        compiler_params=_params(("parallel", "arbitrary")),
        name="moba",
    )(pa, pa, pa)


def _sb_kernel(q_ref, k_ref, v_ref, o_ref, vt_ref, *, seq):
    TQ = SB_TILE
    qb = pl.program_id(1)
    scale = HEAD_DIM ** -0.5
    q = q_ref[...]

    @pl.when(qb == 0)
    def _():
        for c in range(seq // LANES):
            blk = v_ref[c * LANES:(c + 1) * LANES, :].astype(F32)
            vt_ref[:, c * LANES:(c + 1) * LANES] = jnp.concatenate(
                [blk[:, :LANES].T, blk[:, LANES:].T], axis=0).astype(BF16)

    before = lax.broadcasted_iota(jnp.int32, (TQ, TQ), 0) < lax.broadcasted_iota(jnp.int32, (TQ, TQ), 1)
    later = jnp.where(before, 1.0, 0.0).astype(BF16)
    first_rows = lax.broadcasted_iota(jnp.int32, (LANES, 1), 0) < HEAD_DIM
    heads = range(N_HEADS)
    pair_of = [h // PAIR for h in heads]
    lanes_of = lambda x, p: x[:, p * LANES:(p + 1) * LANES]
    qms = [(_head_masked(lanes_of(q, pair_of[h]), h % PAIR == 0) * scale).astype(BF16) for h in heads]

    def tiles(kbs, carry, diagonal_first):
        r0 = [pl.multiple_of(kb * TQ, TQ) for kb in kbs]
        k_j = [k_ref[pl.ds(r, TQ), :] for r in r0]
        vt_j = [vt_ref[:, pl.ds(r, TQ)] for r in r0]
        tails, accs = list(carry[:N_HEADS]), list(carry[N_HEADS:])
        units = [(b, h) for b in range(len(kbs)) for h in heads]
        masked = lambda b: diagonal_first and b == 0
        z2 = {(b, h): _bdot_nt(lanes_of(k_j[b], pair_of[h]), qms[h]) * LOG2E for b, h in units}
        cost = {u: jnp.maximum(z2[u], 0.0) + jnp.log2(1.0 + jnp.exp2(-jnp.abs(z2[u]))) for u in units}
        cost = {u: jnp.where(before, cost[u], 0.0) if masked(u[0]) else cost[u] for u in units}
        after = {u: _dot_const_data(later, cost[u], 2) for u in units}
        w = {}
        for b, h in units:
            w_ = jnp.exp2(z2[b, h] - cost[b, h] - after[b, h] - tails[h])
            w[b, h] = jnp.where(before, w_, 0.0) if masked(b) else w_
            tails[h] = tails[h] + jnp.sum(cost[b, h], axis=0, keepdims=True)
        for b, h in units:
            accs[h] = accs[h] + _bdot(vt_j[b][pair_of[h] * LANES:(pair_of[h] + 1) * LANES, :], w[b, h])
        return tuple(tails) + tuple(accs)

    def live(tails):
        least = tails[0]
        for t_ in tails[1:]:
            least = jnp.minimum(least, t_)
        return jnp.min(least) < SB_DEAD_TAIL

    init = tuple(jnp.zeros((1, TQ), F32) for _ in heads) + tuple(jnp.zeros((LANES, TQ), F32) for _ in heads)
    carry = lax.cond(qb > 0, lambda c: tiles([qb, qb - 1], c, True), lambda c: tiles([qb], c, True), init)

    def step(state):
        i = state[0]
        new = tiles([qb - 1 - i], state[2:], False)
        return (i + 1, live(new[:N_HEADS])) + new

    out = lax.while_loop(lambda state: jnp.logical_and(state[0] < qb, state[1]), step,
                         (jnp.int32(1), live(carry[:N_HEADS])) + carry)
    accs = out[2 + N_HEADS:]
    pairs = [jnp.where(first_rows, accs[PAIR * p], accs[PAIR * p + 1]) for p in range(N_HEADS // PAIR)]
    o_ref[...] = jnp.concatenate(
        [jnp.concatenate([t[:, c * LANES:(c + 1) * LANES].T for c in range(TQ // LANES)], axis=0) for t in pairs],
        axis=1).astype(o_ref.dtype)


def _sb(pa, batch, seq):
    TQ = SB_TILE
    nq = seq // TQ
    W = BRANCH_WIDTH
    return pl.pallas_call(
        functools.partial(_sb_kernel, seq=seq),
        grid=(batch, nq),
        in_specs=[pl.BlockSpec((TQ, W), lambda b, i: (b * nq + i, 3)),
                  pl.BlockSpec((seq, W), lambda b, i: (b, 4)),
                  pl.BlockSpec((seq, W), lambda b, i: (b, 5))],
        out_specs=pl.BlockSpec((TQ, W), lambda b, i: (b * nq + i, 0)),
        out_shape=jax.ShapeDtypeStruct((batch * seq, W), BF16),
        scratch_shapes=[pltpu.VMEM((W, seq), BF16)],
        compiler_params=_params(("parallel", "arbitrary")),
        name="sb",
    )(pa, pa, pa)


def _merge_kernel(x_ref, ya_ref, yb_ref, yc_ref, yd_ref, gpre_ref, wgate_ref, wbr_ref, wout_ref, gpost_ref, o_ref):
    rows = x_ref.shape[0] // TOKEN_SPLIT
    for part in range(TOKEN_SPLIT):
        r = slice(part * rows, (part + 1) * rows)
        x = x_ref[r, :]
        h = _rms(x, gpre_ref[...]).astype(BF16)
        merged = None
        for g, y_ref in enumerate((ya_ref, yb_ref, yc_ref, yd_ref)):
            gate = _sigmoid(_dot(h, wgate_ref[g]))
            term = gate * _dot(y_ref[r, :], wbr_ref[g])
            merged = term if merged is None else merged + term
        mix = _dot(merged.astype(BF16), wout_ref[...])
        o_ref[r, :] = x + _rms(mix, gpost_ref[...])


def _merge(x, ya, yb, yc, yd, gpre, wgate, wbr, wout, gpost):
    n = x.shape[0]
    tm = TOKEN_TILE
    row = lambda w: pl.BlockSpec((tm, w), lambda i: (i, 0))
    return pl.pallas_call(
        _merge_kernel,
        grid=(n // tm,),
        in_specs=[row(D_MODEL)] + [row(BRANCH_WIDTH)] * N_BRANCH
        + [_const_spec(gpre.shape), _const_spec(wgate.shape), _const_spec(wbr.shape),
           _const_spec(wout.shape), _const_spec(gpost.shape)],
        out_specs=row(D_MODEL),
        out_shape=jax.ShapeDtypeStruct((n, D_MODEL), F32),
        compiler_params=_params(("parallel",)),
        name="merge",
    )(x, ya, yb, yc, yd, gpre, wgate, wbr, wout, gpost)


def _ffn_kernel(x_ref, gpre_ref, wup_ref, wdown_ref, gpost_ref, o_ref):
    rows = x_ref.shape[0] // TOKEN_SPLIT
    for part in range(TOKEN_SPLIT):
        r = slice(part * rows, (part + 1) * rows)
        x = x_ref[r, :]
        h = _rms(x, gpre_ref[...]).astype(BF16)
        f = None
        for c in range(D_FF // FF_CHUNK):
            u = _dot(h, wup_ref[:, c * FF_CHUNK:(c + 1) * FF_CHUNK])
            a = jnp.square(jnp.maximum(u, 0.0)).astype(BF16)
            t = _dot(a, wdown_ref[c * FF_CHUNK:(c + 1) * FF_CHUNK, :])
            f = t if f is None else f + t
        o_ref[r, :] = x + _rms(f, gpost_ref[...])


def _ffn(x, gpre, wup, wdown, gpost):
    n = x.shape[0]
    tm = TOKEN_TILE
    row = pl.BlockSpec((tm, D_MODEL), lambda i: (i, 0))
    return pl.pallas_call(
        _ffn_kernel,
        grid=(n // tm,),
        in_specs=[row, _const_spec(gpre.shape), _const_spec(wup.shape), _const_spec(wdown.shape),
                  _const_spec(gpost.shape)],
        out_specs=row,
        out_shape=jax.ShapeDtypeStruct((n, D_MODEL), F32),
        compiler_params=_params(("parallel",)),
        name="ffn",
    )(x, gpre, wup, wdown, gpost)


def _pad_lanes(a, width):
    return jnp.pad(a, ((0, 0), (0, width - a.shape[1])))


def _small_groups(w_in, layer):
    main = 4 * BRANCH_WIDTH
    gdn_in = main + 2 * N_HEADS
    wg_small = _pad_lanes(w_in[layer, :, main:gdn_in], SMALL_PAD)
    ws_small = _pad_lanes(w_in[layer, :, gdn_in + ATT_W + main:], SMALL_PAD)
    return wg_small.astype(BF16), ws_small.astype(BF16)


def kernel(x, norm_mix_pre, norm_mix_post, norm_ffn_pre, norm_ffn_post, w_in, gdn_conv, gdn_a_log, gdn_dt_bias,
           gdn_norm, ssd_conv, ssd_conv_bias, ssd_a_log, ssd_dt_bias, ssd_d, ssd_norm, w_gate, w_branch, w_out,
           w_up, w_down):
    batch, seq, d = x.shape
    assert d == D_MODEL and seq % MOBA_BLOCK == 0 and (batch * seq) % TOKEN_TILE == 0
    depth = w_in.shape[0]
    xf = x.reshape(batch * seq, d).astype(F32)
    row = lambda a: a.reshape(1, -1).astype(F32)
    for l in range(depth):
        wg_small, ws_small = _small_groups(w_in, l)
        pg, pa, ps = _in_proj(xf, row(norm_mix_pre[l]), w_in.astype(F32), l, wg_small, ws_small)
        gdn_a = _pad_lanes(jnp.concatenate([jnp.zeros((N_HEADS,), F32), gdn_a_log[l]]).reshape(1, -1), SMALL_PAD)
        gdn_b = _pad_lanes(jnp.concatenate([jnp.zeros((N_HEADS,), F32), gdn_dt_bias[l]]).reshape(1, -1), SMALL_PAD)
        ya = _gdn(pg, gdn_conv[l].astype(F32), gdn_a, gdn_b, row(jnp.tile(gdn_norm[l], N_HEADS)), batch, seq)
        yb = _moba(pa, batch, seq)
        yc = _sb(pa, batch, seq)
        yd = _ssd(ps, ssd_conv[l].astype(F32), row(ssd_conv_bias[l]),
                  _pad_lanes(row(ssd_a_log[l]), SMALL_PAD), _pad_lanes(row(ssd_dt_bias[l]), SMALL_PAD),
                  row(jnp.repeat(ssd_d[l], HEAD_DIM)), row(ssd_norm[l]), batch, seq)
        xf = _merge(xf, ya, yb, yc, yd, row(norm_mix_pre[l]), w_gate[l].astype(BF16), w_branch[l].astype(BF16),
                    w_out[l].astype(BF16), row(norm_mix_post[l]))
        xf = _ffn(xf, row(norm_ffn_pre[l]), w_up[l].astype(BF16), w_down[l].astype(BF16), row(norm_ffn_post[l]))
    return xf.reshape(batch, seq, d).astype(x.dtype)
```

```python
import functools

import jax
import jax.numpy as jnp
from jax import lax
from jax.experimental import pallas as pl
from jax.experimental.pallas import tpu as pltpu

F32 = jnp.float32
BF16 = jnp.bfloat16
HIGHEST = lax.Precision.HIGHEST

D_MODEL = 1024
N_BRANCH = 4
BRANCH_WIDTH = 256
HEAD_DIM = 64
N_HEADS = 4
CONV_WIDTH = 4
GDN_CHUNK = 64
MOBA_BLOCK = 256
MOBA_TOPK = 3
SSD_STATE = 128
SSD_GROUPS = 2
SSD_CHUNK = 128
D_FF = 4 * D_MODEL
EPS = 1e-6

LANES = 128
SUBLANES = 8
PAIR = LANES // HEAD_DIM
SMALL_PAD = LANES
CONV_HALO = 8
TOKEN_TILE = 512
PART_ROWS = 256
TOKEN_SPLIT = 4
FF_CHUNK = 1024
SB_TILE = 256
LOG2E = 1.4426950408889634
SB_DEAD_TAIL = 256.0
SSD_UNROLL = 4
GDN_GROUP = 2
GDN_UNROLL = 8
VMEM_LIMIT = 56 * 1024 * 1024

GDN_W = 3 * BRANCH_WIDTH + BRANCH_WIDTH + SMALL_PAD
ATT_W = 6 * BRANCH_WIDTH
SSD_CONV_DIM = BRANCH_WIDTH + 2 * SSD_GROUPS * SSD_STATE
SSD_W = SSD_CONV_DIM + BRANCH_WIDTH + SMALL_PAD


def _dot(a, b):
    return jnp.dot(a, b, preferred_element_type=F32)


def _bdot(a, b):
    return _dot(a.astype(BF16), b.astype(BF16))


def _bdot_nt(a, b):
    return lax.dot_general(a.astype(BF16), b.astype(BF16), (((1,), (1,)), ((), ())),
                           preferred_element_type=F32)


def _hdot(a, b):
    return jnp.dot(a, b, preferred_element_type=F32, precision=HIGHEST)


def _split(x, pieces):
    out = []
    for _ in range(pieces - 1):
        p = x.astype(BF16)
        out.append(p)
        x = x - p.astype(F32)
    out.append(x.astype(BF16))
    return out


def _dot_data_const(x, m, pieces):
    acc = None
    for p in _split(x, pieces):
        t = _dot(p, m)
        acc = t if acc is None else acc + t
    return acc


def _dot_const_data(m, x, pieces):
    acc = None
    for p in _split(x, pieces):
        t = _dot(m, p)
        acc = t if acc is None else acc + t
    return acc


def _sigmoid(x):
    return 1.0 / (1.0 + jnp.exp(-x))


def _silu(x):
    return x * _sigmoid(x)


def _softplus(x):
    return jnp.maximum(x, 0.0) + jnp.log(1.0 + jnp.exp(-jnp.abs(x)))


def _rms(x, gain):
    return x * lax.rsqrt(jnp.mean(x * x, axis=-1, keepdims=True) + EPS) * gain


def _params(sem):
    return pltpu.CompilerParams(dimension_semantics=sem, vmem_limit_bytes=VMEM_LIMIT)


def _const_spec(shape):
    nd = len(shape)
    return pl.BlockSpec(shape, lambda *_: (0,) * nd, pipeline_mode=pl.Buffered(1))


def _cols(w_ref, start, width):
    base = start // LANES * LANES
    stop = min(-(-(start + width) // LANES) * LANES, w_ref.shape[1])
    return w_ref[:, base:stop][:, start - base:start - base + width].astype(BF16)


def _in_proj_kernel(x_ref, gain_ref, w_ref, wgs_ref, wss_ref, og_ref, oa_ref, os_ref, wg_ref, wa_ref, ws_ref):
    W = BRANCH_WIDTH
    main = 4 * W

    @pl.when(pl.program_id(0) == 0)
    def _():
        off_att = main + 2 * N_HEADS
        off_ssd = off_att + ATT_W
        for c in range(main // W):
            wg_ref[:, c * W:(c + 1) * W] = _cols(w_ref, c * W, W)
        for c in range(ATT_W // W):
            wa_ref[:, c * W:(c + 1) * W] = _cols(w_ref, off_att + c * W, W)
        for c in range(SSD_CONV_DIM // W):
            ws_ref[:, c * W:(c + 1) * W] = _cols(w_ref, off_ssd + W + c * W, W)
        ws_ref[:, SSD_CONV_DIM:main] = _cols(w_ref, off_ssd, W)

    h = _rms(x_ref[...], gain_ref[...]).astype(BF16)
    og_ref[:, :main] = _dot(h, wg_ref[...])
    og_ref[:, main:] = _dot(h, wgs_ref[...])
    oa_ref[...] = _dot(h, wa_ref[...]).astype(BF16)
    os_ref[:, :main] = _dot(h, ws_ref[...])
    os_ref[:, main:] = _dot(h, wss_ref[...])


def _in_proj(x, gain, w_in, layer, wg_small, ws_small):
    n = x.shape[0]
    tm = TOKEN_TILE
    row = lambda w: pl.BlockSpec((tm, w), lambda i: (i, 0))
    main = 4 * BRANCH_WIDTH
    return pl.pallas_call(
        _in_proj_kernel,
        grid=(n // tm,),
        in_specs=[row(D_MODEL), _const_spec((1, D_MODEL)),
                  pl.BlockSpec((None,) + w_in.shape[1:], lambda i: (layer, 0, 0), pipeline_mode=pl.Buffered(1)),
                  _const_spec(wg_small.shape), _const_spec(ws_small.shape)],
        out_specs=[row(GDN_W), row(ATT_W), row(SSD_W)],
        out_shape=[jax.ShapeDtypeStruct((n, GDN_W), F32), jax.ShapeDtypeStruct((n, ATT_W), BF16),
                   jax.ShapeDtypeStruct((n, SSD_W), F32)],
        scratch_shapes=[pltpu.VMEM((D_MODEL, main), BF16), pltpu.VMEM((D_MODEL, ATT_W), BF16),
                        pltpu.VMEM((D_MODEL, main), BF16)],
        compiler_params=_params(("arbitrary",)),
        name="in_proj",
    )(x, gain, w_in, wg_small, ws_small)


def _conv_silu_into(src_ref, w_ref, bias, dst_ref, seq, chunk):
    w = w_ref[...]
    taps = [w[k:k + 1, :] for k in range(CONV_WIDTH)]

    def finish(acc):
        if bias is not None:
            acc = acc + bias
        return _silu(acc)

    blk = src_ref[0:chunk, :]
    rows = lax.broadcasted_iota(jnp.int32, (chunk, 1), 0)
    acc = taps[CONV_WIDTH - 1] * blk
    for k in range(CONV_WIDTH - 1):
        shift = CONV_WIDTH - 1 - k
        acc = acc + taps[k] * jnp.where(rows >= shift, pltpu.roll(blk, shift, 0), 0.0)
    dst_ref[0:chunk, :] = finish(acc)

    def body(c, carry):
        r0 = pl.multiple_of(c * chunk, chunk)
        blk = src_ref[pl.ds(r0 - CONV_HALO, chunk + CONV_HALO), :]
        acc = taps[CONV_WIDTH - 1] * blk[CONV_HALO:]
        for k in range(CONV_WIDTH - 1):
            acc = acc + taps[k] * pltpu.roll(blk, CONV_WIDTH - 1 - k, 0)[CONV_HALO:]
        dst_ref[pl.ds(r0, chunk), :] = finish(acc)
        return carry

    lax.fori_loop(1, seq // chunk, body, 0)


def _block_diag(x, mask):
    return jnp.where(mask, jnp.concatenate([x] * N_HEADS, axis=0), jnp.zeros((), x.dtype))


def _dot1_bd(a, b, mask):
    return _dot(a.astype(BF16), _block_diag(b.astype(BF16), mask))


def _dot3_bd(a, b, mask):
    ah, al = _split(a, 2)
    bh, bl = _split(b, 2)
    bdh = _block_diag(bh, mask)
    return _dot(ah, bdh) + _dot(al, bdh) + _dot(ah, _block_diag(bl, mask))


def _gdn_prepare_kernel(qkv_ref, bg_ref, convw_ref, alog_ref, dtb_ref,
                        u_ref, wq_ref, aqk_ref, kdt_ref, gl_ref, conv_ref, *, seq):
    C = GDN_CHUNK
    W = BRANCH_WIDTH
    NC = seq // C
    _conv_silu_into(qkv_ref, convw_ref, None, conv_ref, seq, 4 * C)

    def iota(shape, dim):
        return lax.broadcasted_iota(jnp.int32, shape, dim)

    same_head = (iota((W, W), 0) // HEAD_DIM) == (iota((W, W), 1) // HEAD_DIM)
    same_head2 = jnp.concatenate([same_head, same_head], axis=1)
    seg_ones = jnp.where(same_head, 1.0, 0.0).astype(BF16)
    r_c = iota((C, W), 0)
    j_c = iota((C, W), 1) % HEAD_DIM
    incl = r_c >= j_c
    strict = r_c > j_c
    eye = jnp.where(r_c == j_c, 1.0, 0.0)
    blocks = [(r_c // n) == (j_c // n) for n in (SUBLANES, 2 * SUBLANES, 4 * SUBLANES, C)]
    tril = jnp.where(iota((C, C), 0) >= iota((C, C), 1), 1.0, 0.0).astype(BF16)
    lane = iota((1, LANES), 1)
    ex_r = iota((LANES, 2 * W), 0)
    ex_c = iota((LANES, 2 * W), 1)
    expand = jnp.where(ex_r == jnp.where(ex_c < W, ex_c // HEAD_DIM, N_HEADS + (ex_c - W) // HEAD_DIM),
                       1.0, 0.0).astype(BF16)
    neg_a = -jnp.exp(alog_ref[...])
    dtb = dtb_ref[...]
    scale = HEAD_DIM ** -0.5

    def row_form(gc):
        t = jnp.concatenate([gc, jnp.zeros((LANES - C, LANES), F32)], axis=0).T
        heads_rows = [t[N_HEADS + h:N_HEADS + h + 1, :] for h in range(N_HEADS)]
        tiles = [jnp.where(lane < HEAD_DIM, heads_rows[PAIR * p_], pltpu.roll(heads_rows[PAIR * p_ + 1], HEAD_DIM, 1))
                 for p_ in range(N_HEADS // PAIR)]
        return jnp.broadcast_to(jnp.concatenate(tiles, axis=1), (C, W))

    def prepare(i, carry):
        cs = [i * GDN_UNROLL + n for n in range(GDN_UNROLL)]
        r0s = [pl.multiple_of(c * C, C) for c in cs]
        each = lambda f, *xs: [f(*a) for a in zip(*xs)]
        qkv = [conv_ref[pl.ds(r0, C), :] for r0 in r0s]
        q = [a[:, :W] for a in qkv]
        k = [a[:, W:2 * W] for a in qkv]
        v = [a[:, 2 * W:] for a in qkv]
        ss = each(lambda q_, k_: _dot_data_const(jnp.concatenate([q_ * q_, k_ * k_], axis=0), seg_ones, 2), q, k)
        qn = each(lambda q_, s_: q_ * lax.rsqrt(s_[:C] + EPS) * scale, q, ss)
        kn = each(lambda k_, s_: k_ * lax.rsqrt(s_[C:] + EPS), k, ss)
        bg = [bg_ref[pl.ds(r0, C), :] for r0 in r0s]
        gc = each(lambda b_: _dot_const_data(tril, neg_a * _softplus(b_ + dtb), 3), bg)
        ex = each(lambda b_, g_: _dot_data_const(jnp.where(lane < N_HEADS, _sigmoid(b_), g_), expand, 3), bg, gc)
        beta = [a[:, :W] for a in ex]
        gcc = [a[:, W:] for a in ex]
        gcr = each(row_form, gc)
        dm = each(lambda a_, b_: jnp.exp(jnp.where(incl, a_ - b_, -jnp.inf)), gcc, gcr)
        kb = each(lambda a_, b_: a_ * b_, kn, beta)
        vb = each(lambda a_, b_: a_ * b_, v, beta)
        qk = each(lambda q_, kb_, kn_: lax.dot_general(
            jnp.concatenate([q_, kb_], axis=0).astype(BF16), _block_diag(kn_.astype(BF16), same_head),
            (((1,), (1,)), ((), ())), preferred_element_type=F32), qn, kb, kn)
        a_qk = each(lambda x_, d_: x_[:C] * d_, qk, dm)
        a = each(lambda x_, d_: jnp.where(strict, x_[C:] * d_, 0.0), qk, dm)
        d = [jnp.where(blocks[0], a_, 0.0) for a_ in a]
        p = [eye - d_ for d_ in d]
        b = each(lambda d_: _dot3_bd(d_, d_, same_head), d)
        pb = each(lambda p_, b_: _dot3_bd(jnp.concatenate([p_, b_], axis=0), b_, same_head), p, b)
        p = each(lambda p_, x_: p_ + x_[:C], p, pb)
        p = each(lambda p_, x_: p_ + _dot3_bd(p_, x_[C:], same_head), p, pb)
        for inner, outer in zip(blocks[:-1], blocks[1:]):
            pe = each(lambda p_, a_: _dot1_bd(p_, jnp.where(outer & ~inner, a_, 0.0), same_head), p, a)
            p = each(lambda p_, x_: p_ - _dot1_bd(x_, p_, same_head), p, pe)
        t_inv = p
        e_gc = [jnp.exp(g_) for g_ in gcc]
        uw = each(lambda t_, vb_, kb_, e_: _dot3_bd(t_, jnp.concatenate([vb_, kb_ * e_], axis=1), same_head2),
                  t_inv, vb, kb, e_gc)
        for n, c in enumerate(cs):
            r0 = r0s[n]
            g_last = gcc[n][C - 1:C, :]
            u_ref[pl.ds(r0, C), :] = uw[n][:, :W]
            wq_ref[pl.ds(pl.multiple_of(c * 2 * C, 2 * C), 2 * C), :] = jnp.concatenate(
                [uw[n][:, W:], qn[n] * e_gc[n]], axis=0).astype(BF16)
            aqk_ref[pl.ds(r0, C), :] = a_qk[n].astype(BF16)
            k_dec = jnp.concatenate([kn[n] * jnp.exp(g_last - gcc[n]), jnp.zeros((LANES - C, W), F32)], axis=0)
            kdt_ref[pl.ds(pl.multiple_of(c * W, W), W), :] = jnp.concatenate(
                [k_dec[:, :LANES].T, k_dec[:, LANES:].T], axis=0).astype(BF16)
            gl_ref[c] = jnp.broadcast_to(jnp.exp(g_last), (SUBLANES, W))
        return carry

    lax.fori_loop(0, NC // GDN_UNROLL, prepare, 0)


def _gdn_recur_kernel(u_ref, wq_ref, aqk_ref, kdt_ref, gl_ref, z_ref, norm_ref, o_ref, s_ref, *, seq, group):
    C = GDN_CHUNK
    W = BRANCH_WIDTH
    NC = seq // C
    same_head = ((lax.broadcasted_iota(jnp.int32, (W, W), 0) // HEAD_DIM)
                 == (lax.broadcasted_iota(jnp.int32, (W, W), 1) // HEAD_DIM))
    seg_ones = jnp.where(same_head, 1.0, 0.0).astype(BF16)
    gain = norm_ref[...]
    seqs = range(group)
    s_ref[...] = jnp.zeros_like(s_ref)

    def recur(c, carry):
        rows = [pl.multiple_of(g * seq + c * C, C) for g in seqs]
        rows2 = [pl.multiple_of(2 * (g * seq + c * C), 2 * C) for g in seqs]
        rows_k = [pl.multiple_of((g * NC + c) * W, W) for g in seqs]
        s = [s_ref[g] for g in seqs]
        ws_qs = [_dot(wq_ref[pl.ds(rows2[g], 2 * C), :], s[g].astype(BF16)) for g in seqs]
        v_new = [(u_ref[pl.ds(rows[g], C), :] - ws_qs[g][:C]).astype(BF16) for g in seqs]
        upd = [_dot(kdt_ref[pl.ds(rows_k[g], W), :],
                    jnp.concatenate([v_new[g], jnp.zeros((LANES - C, W), BF16)], axis=0)) for g in seqs]
        for g in seqs:
            s_ref[g] = s[g] * gl_ref[g * NC + c][0:1, :] + jnp.where(same_head, upd[g], 0.0)
        o = [ws_qs[g][C:] + _dot(aqk_ref[pl.ds(rows[g], C), :], _block_diag(v_new[g], same_head)) for g in seqs]
        ms = [_dot_data_const(o_ * o_, seg_ones, 2) * (1.0 / HEAD_DIM) for o_ in o]
        for g in seqs:
            y = o[g] * lax.rsqrt(ms[g] + EPS) * gain * _silu(z_ref[pl.ds(rows[g], C), :])
            o_ref[pl.ds(rows[g], C), :] = y.astype(o_ref.dtype)
        return carry

    lax.fori_loop(0, NC, recur, 0)


def _gdn(pg, conv_w, a_row, dtb_row, norm_row, batch, seq):
    C = GDN_CHUNK
    W = BRANCH_WIDTH
    nc = seq // C
    n = batch * seq
    u, wq, aqk, kdt, gl = pl.pallas_call(
        functools.partial(_gdn_prepare_kernel, seq=seq),
        grid=(batch,),
        in_specs=[pl.BlockSpec((seq, 3 * W), lambda b: (b, 0)),
                  pl.BlockSpec((seq, SMALL_PAD), lambda b: (b, 4 * W // SMALL_PAD)),
                  _const_spec(conv_w.shape), _const_spec(a_row.shape), _const_spec(dtb_row.shape)],
        out_specs=[pl.BlockSpec((seq, W), lambda b: (b, 0)),
                   pl.BlockSpec((2 * seq, W), lambda b: (b, 0)),
                   pl.BlockSpec((seq, W), lambda b: (b, 0)),
                   pl.BlockSpec((nc * W, LANES), lambda b: (b, 0)),
                   pl.BlockSpec((nc, SUBLANES, W), lambda b: (b, 0, 0))],
        out_shape=[jax.ShapeDtypeStruct((n, W), F32),
                   jax.ShapeDtypeStruct((2 * n, W), BF16),
                   jax.ShapeDtypeStruct((n, W), BF16),
                   jax.ShapeDtypeStruct((batch * nc * W, LANES), BF16),
                   jax.ShapeDtypeStruct((batch * nc, SUBLANES, W), F32)],
        scratch_shapes=[pltpu.VMEM((seq, 3 * W), F32)],
        compiler_params=_params(("parallel",)),
        name="gdn_prepare",
    )(pg, pg, conv_w, a_row, dtb_row)
    group = GDN_GROUP if batch % GDN_GROUP == 0 else 1
    return pl.pallas_call(
        functools.partial(_gdn_recur_kernel, seq=seq, group=group),
        grid=(batch // group,),
        in_specs=[pl.BlockSpec((group * seq, W), lambda b: (b, 0)),
                  pl.BlockSpec((2 * group * seq, W), lambda b: (b, 0)),
                  pl.BlockSpec((group * seq, W), lambda b: (b, 0)),
                  pl.BlockSpec((group * nc * W, LANES), lambda b: (b, 0)),
                  pl.BlockSpec((group * nc, SUBLANES, W), lambda b: (b, 0, 0)),
                  pl.BlockSpec((group * seq, W), lambda b: (b, 3)),
                  _const_spec(norm_row.shape)],
        out_specs=pl.BlockSpec((group * seq, W), lambda b: (b, 0)),
        out_shape=jax.ShapeDtypeStruct((n, W), BF16),
        scratch_shapes=[pltpu.VMEM((group, W, W), F32)],
        compiler_params=_params(("parallel",)),
        name="gdn_recur",
    )(u, wq, aqk, kdt, gl, pg, norm_row)


def _ssd_kernel(xbc_ref, z_ref, dt_ref, convw_ref, convb_ref, alog_ref, dtb_ref, dskip_ref, norm_ref,
                o_ref, conv_ref, s_ref, *, seq):
    L = SSD_CHUNK
    W = BRANCH_WIDTH
    GN = SSD_GROUPS * SSD_STATE
    _conv_silu_into(xbc_ref, convw_ref, convb_ref[...], conv_ref, seq, L)
    s_ref[...] = jnp.zeros_like(s_ref)

    ri = lax.broadcasted_iota(jnp.int32, (L, L), 0)
    ci = lax.broadcasted_iota(jnp.int32, (L, L), 1)
    incl = ri >= ci
    tril = jnp.where(incl, 1.0, 0.0)
    first = lax.broadcasted_iota(jnp.int32, (1, LANES), 1) < HEAD_DIM
    neg_a = -jnp.exp(alog_ref[...])
    dtb = dtb_ref[...]
    dskip = dskip_ref[...]
    gain = norm_ref[...]

    groups = range(SSD_GROUPS)
    units = [(n, g) for n in range(SSD_UNROLL) for g in groups]
    pair_col = lambda t, g: jnp.where(first, t[:, PAIR * g:PAIR * g + 1], t[:, PAIR * g + 1:PAIR * g + 2])

    def body(i, carry):
        r0 = [pl.multiple_of((i * SSD_UNROLL + n) * L, L) for n in range(SSD_UNROLL)]
        xbc = [conv_ref[pl.ds(r, L), :] for r in r0]
        dt = [_softplus(dt_ref[pl.ds(r, L), :] + dtb) for r in r0]
        acs = [_hdot(tril, dt_ * neg_a) for dt_ in dt]
        acs_rows = [a_.T for a_ in acs]
        wide = lambda t, h: jnp.broadcast_to(t[:, h:h + 1], (L, LANES))
        acs_w = {(n, h): wide(acs[n], h) for n in range(SSD_UNROLL) for h in range(N_HEADS)}
        acs_p = {(n, g): jnp.where(first, acs_w[n, PAIR * g], acs_w[n, PAIR * g + 1]) for n, g in units}
        dt_p = {(n, g): jnp.where(first, wide(dt[n], PAIR * g), wide(dt[n], PAIR * g + 1)) for n, g in units}
        a_last = {u: acs_p[u][L - 1:L, :] for u in units}
        x2 = {(n, g): xbc[n][:, g * LANES:(g + 1) * LANES] for n, g in units}
        bm = {(n, g): xbc[n][:, W + g * SSD_STATE:W + (g + 1) * SSD_STATE] for n, g in units}
        cm = {(n, g): xbc[n][:, W + GN + g * SSD_STATE:W + GN + (g + 1) * SSD_STATE] for n, g in units}
        cb = {u: _bdot_nt(cm[u], bm[u]) for u in units}
        grow = {u: _bdot(bm[u].T, x2[u] * (dt_p[u] * jnp.exp(a_last[u] - acs_p[u]))) for u in units}
        s = {g: s_ref[g] for g in groups}
        y_off = {}
        for n, g in units:
            y_off[n, g] = _bdot(cm[n, g], s[g]) * jnp.exp(acs_p[n, g])
            s[g] = s[g] * jnp.exp(a_last[n, g]) + grow[n, g]
        for g in groups:
            s_ref[g] = s[g]
        xdt = {u: x2[u] * dt_p[u] for u in units}
        yd = {}
        for n, g in units:
            for h in (PAIR * g, PAIR * g + 1):
                lmat = jnp.exp(jnp.where(incl, acs_w[n, h] - acs_rows[n][h:h + 1, :], -jnp.inf))
                yd[n, h] = _bdot(cb[n, g] * lmat, xdt[n, g])
        for n in range(SSD_UNROLL):
            ys = [jnp.where(first, yd[n, PAIR * g], yd[n, PAIR * g + 1]) + y_off[n, g] for g in groups]
            y = jnp.concatenate(ys, axis=1) + dskip * xbc[n][:, :W]
            zt = z_ref[pl.ds(r0[n], L), :]
            o_ref[pl.ds(r0[n], L), :] = _rms(y * _silu(zt), gain).astype(o_ref.dtype)
        return carry

    lax.fori_loop(0, seq // (L * SSD_UNROLL), body, 0)


def _ssd(ps, conv_w, conv_b, a_row, dtb_row, dskip_row, norm_row, batch, seq):
    kern = functools.partial(_ssd_kernel, seq=seq)
    return pl.pallas_call(
        kern,
        grid=(batch,),
        in_specs=[pl.BlockSpec((seq, SSD_CONV_DIM), lambda b: (b, 0)),
                  pl.BlockSpec((seq, BRANCH_WIDTH), lambda b: (b, SSD_CONV_DIM // BRANCH_WIDTH)),
                  pl.BlockSpec((seq, SMALL_PAD), lambda b: (b, (SSD_CONV_DIM + BRANCH_WIDTH) // SMALL_PAD)),
                  _const_spec(conv_w.shape), _const_spec(conv_b.shape), _const_spec(a_row.shape),
                  _const_spec(dtb_row.shape), _const_spec(dskip_row.shape), _const_spec(norm_row.shape)],
        out_specs=pl.BlockSpec((seq, BRANCH_WIDTH), lambda b: (b, 0)),
        out_shape=jax.ShapeDtypeStruct((batch * seq, BRANCH_WIDTH), BF16),
        scratch_shapes=[pltpu.VMEM((seq, SSD_CONV_DIM), F32),
                        pltpu.VMEM((SSD_GROUPS, SSD_STATE, LANES), F32)],
        compiler_params=_params(("parallel",)),
        name="ssd",
    )(ps, ps, ps, conv_w, conv_b, a_row, dtb_row, dskip_row, norm_row)


def _head_masked(q, first_half):
    lane = lax.broadcasted_iota(jnp.int32, (1, LANES), 1)
    keep = (lane < HEAD_DIM) if first_half else (lane >= HEAD_DIM)
    return jnp.where(keep, q.astype(F32), 0.0)


def _moba_kernel(q_ref, k_ref, v_ref, o_ref, kmean_ref, vt_ref, *, seq):
    BS = MOBA_BLOCK
    NB = seq // BS
    W = BRANCH_WIDTH
    qb = pl.program_id(1)
    scale = HEAD_DIM ** -0.5
    q = q_ref[...]

    @pl.when(qb == 0)
    def _():
        kmean = jnp.sum(k_ref[...].astype(F32).reshape(NB, BS, W), axis=1) * (1.0 / BS)
        kmean_ref[...] = jnp.concatenate([kmean, jnp.zeros((SUBLANES, W), F32)], axis=0)
        for c in range(seq // LANES):
            blk = v_ref[c * LANES:(c + 1) * LANES, :].astype(F32)
            vt_ref[:, c * LANES:(c + 1) * LANES] = jnp.concatenate(
                [blk[:, :LANES].T, blk[:, LANES:].T], axis=0).astype(BF16)

    rel = (lax.broadcasted_iota(jnp.int32, (BS, BS), 1)
           - lax.broadcasted_iota(jnp.int32, (BS, BS), 0)).astype(F32)
    first_rows = lax.broadcasted_iota(jnp.int32, (LANES, 1), 0) < HEAD_DIM
    blk_id = lax.broadcasted_iota(jnp.int32, (SUBLANES, BS), 0)
    heads = range(N_HEADS)
    pair_of = [h // PAIR for h in heads]
    lanes_of = lambda x, p: x[:, p * LANES:(p + 1) * LANES]
    slopes = [2.0 ** (-8.0 * (h + 1) / N_HEADS) for h in heads]

    qf = [_head_masked(lanes_of(q, pair_of[h]), h % PAIR == 0) for h in heads]
    km = _split(kmean_ref[...], 3)
    gate = []
    for h in heads:
        qh = qf[h].astype(BF16)
        g_h = None
        for piece in km:
            t = lax.dot_general(lanes_of(piece, pair_of[h]), qh, (((1,), (1,)), ((), ())),
                                preferred_element_type=F32)
            g_h = t if g_h is None else g_h + t
        gate.append(g_h[:NB])
    sels = []
    for h in heads:
        rank = jnp.zeros((NB, BS), F32)
        for i in range(NB):
            gi = gate[h][i:i + 1, :]
            beats = jnp.where(gi > gate[h], 1.0, jnp.where((gi == gate[h]) & (blk_id > i), 1.0, 0.0))
            rank = rank + jnp.where(i < qb, beats, 0.0)
        sels.append(jnp.where((rank < MOBA_TOPK) & (blk_id < qb), 1.0, 0.0))
    qms = [(qf[h] * scale).astype(BF16) for h in heads]

    def blocks(js, carry, own_first):
        r0 = [pl.multiple_of(j * BS, BS) for j in js]
        k_j = [k_ref[pl.ds(r, BS), :] for r in r0]
        vt_j = [vt_ref[:, pl.ds(r, BS)] for r in r0]
        is_own = lambda b: own_first and b == 0
        dist = [rel if is_own(b) else rel + ((qb - j) * BS).astype(F32) for b, j in enumerate(js)]
        m, l, acc = carry[:N_HEADS], carry[N_HEADS:2 * N_HEADS], carry[2 * N_HEADS:]
        units = [(b, h) for b in range(len(js)) for h in heads]
        raw = {u: _bdot_nt(lanes_of(k_j[u[0]], pair_of[u[1]]), qms[u[1]]) for u in units}
        keep = {(b, h): (rel >= 0) if is_own(b) else
                jnp.sum(jnp.where(blk_id == js[b], sels[h], 0.0), axis=0, keepdims=True) > 0.5 for b, h in units}
        s = {(b, h): jnp.where(keep[b, h], raw[b, h] - slopes[h] * dist[b], -jnp.inf) for b, h in units}
        m_new = list(m)
        for b, h in units:
            m_blk = jnp.max(s[b, h], axis=0, keepdims=True)
            m_new[h] = m_blk if m_new[h] is None else jnp.maximum(m_new[h], m_blk)
        p = {(b, h): jnp.exp(s[b, h] - m_new[h]) for b, h in units}
        pv, l_new = [None] * N_HEADS, [None] * N_HEADS
        for b, h in units:
            t = _bdot(vt_j[b][pair_of[h] * LANES:(pair_of[h] + 1) * LANES, :], p[b, h])
            pv[h] = t if pv[h] is None else pv[h] + t
            t = jnp.sum(p[b, h], axis=0, keepdims=True)
            l_new[h] = t if l_new[h] is None else l_new[h] + t
        if not own_first:
            alpha = [jnp.exp(m[h] - m_new[h]) for h in heads]
            l_new = [l[h] * alpha[h] + l_new[h] for h in heads]
            pv = [acc[h] * alpha[h] + pv[h] for h in heads]
        return tuple(m_new) + tuple(l_new) + tuple(pv)

    none = (None,) * (3 * N_HEADS)
    state = lax.cond(qb >= 2, lambda: blocks([qb, 0, 1], none, True), lambda: blocks([qb], none, True))
    out = lax.fori_loop(1, qb // 2, lambda i, c: blocks([2 * i, 2 * i + 1], c, False), state)
    out = lax.cond(qb % 2 == 1, lambda c: blocks([qb - 1], c, False), lambda c: c, out)
    l, acc = out[N_HEADS:2 * N_HEADS], out[2 * N_HEADS:]
    o = [acc[h] / l[h] for h in heads]
    pairs = [jnp.where(first_rows, o[PAIR * p_], o[PAIR * p_ + 1]) for p_ in range(N_HEADS // PAIR)]
    o_ref[...] = jnp.concatenate(
        [jnp.concatenate([t[:, :LANES].T, t[:, LANES:].T], axis=0) for t in pairs], axis=1).astype(o_ref.dtype)


def _moba(pa, batch, seq):
    BS = MOBA_BLOCK
    nq = seq // BS
    assert nq == SUBLANES
    W = BRANCH_WIDTH
    kern = functools.partial(_moba_kernel, seq=seq)
    return pl.pallas_call(
        kern,
        grid=(batch, nq),
        in_specs=[pl.BlockSpec((BS, W), lambda b, i: (b * nq + i, 0)),
                  pl.BlockSpec((seq, W), lambda b, i: (b, 1)),
                  pl.BlockSpec((seq, W), lambda b, i: (b, 2))],
        out_specs=pl.BlockSpec((BS, W), lambda b, i: (b * nq + i, 0)),
        out_shape=jax.ShapeDtypeStruct((batch * seq, W), BF16),
        scratch_shapes=[pltpu.VMEM((2 * SUBLANES, W), F32),
                        pltpu.VMEM((W, seq), BF16)],
        compiler_params=_params(("parallel", "arbitrary")),
        name="moba",
    )(pa, pa, pa)


def _sb_kernel(q_ref, k_ref, v_ref, o_ref, vt_ref, *, seq):
    TQ = SB_TILE
    qb = pl.program_id(1)
    scale = HEAD_DIM ** -0.5
    q = q_ref[...]

    @pl.when(qb == 0)
    def _():
        for c in range(seq // LANES):
            blk = v_ref[c * LANES:(c + 1) * LANES, :].astype(F32)
            vt_ref[:, c * LANES:(c + 1) * LANES] = jnp.concatenate(
                [blk[:, :LANES].T, blk[:, LANES:].T], axis=0).astype(BF16)

    before = lax.broadcasted_iota(jnp.int32, (TQ, TQ), 0) < lax.broadcasted_iota(jnp.int32, (TQ, TQ), 1)
    later = jnp.where(before, 1.0, 0.0).astype(BF16)
    first_rows = lax.broadcasted_iota(jnp.int32, (LANES, 1), 0) < HEAD_DIM
    heads = range(N_HEADS)
    pair_of = [h // PAIR for h in heads]
    lanes_of = lambda x, p: x[:, p * LANES:(p + 1) * LANES]
    qms = [(_head_masked(lanes_of(q, pair_of[h]), h % PAIR == 0) * scale).astype(BF16) for h in heads]

    def tiles(kbs, carry, diagonal_first):
        r0 = [pl.multiple_of(kb * TQ, TQ) for kb in kbs]
        k_j = [k_ref[pl.ds(r, TQ), :] for r in r0]
        vt_j = [vt_ref[:, pl.ds(r, TQ)] for r in r0]
        tails, accs = list(carry[:N_HEADS]), list(carry[N_HEADS:])
        units = [(b, h) for b in range(len(kbs)) for h in heads]
        masked = lambda b: diagonal_first and b == 0
        z2 = {(b, h): _bdot_nt(lanes_of(k_j[b], pair_of[h]), qms[h]) * LOG2E for b, h in units}
        cost = {u: jnp.maximum(z2[u], 0.0) + jnp.log2(1.0 + jnp.exp2(-jnp.abs(z2[u]))) for u in units}
        cost = {u: jnp.where(before, cost[u], 0.0) if masked(u[0]) else cost[u] for u in units}
        after = {u: _dot_const_data(later, cost[u], 2) for u in units}
        w = {}
        for b, h in units:
            w_ = jnp.exp2(z2[b, h] - cost[b, h] - after[b, h] - tails[h])
            w[b, h] = jnp.where(before, w_, 0.0) if masked(b) else w_
            tails[h] = tails[h] + jnp.sum(cost[b, h], axis=0, keepdims=True)
        for b, h in units:
            accs[h] = accs[h] + _bdot(vt_j[b][pair_of[h] * LANES:(pair_of[h] + 1) * LANES, :], w[b, h])
        return tuple(tails) + tuple(accs)

    def live(tails):
        least = tails[0]
        for t_ in tails[1:]:
            least = jnp.minimum(least, t_)
        return jnp.min(least) < SB_DEAD_TAIL

    init = tuple(jnp.zeros((1, TQ), F32) for _ in heads) + tuple(jnp.zeros((LANES, TQ), F32) for _ in heads)
    carry = lax.cond(qb > 0, lambda c: tiles([qb, qb - 1], c, True), lambda c: tiles([qb], c, True), init)

    def step(state):
        i = state[0]
        new = tiles([qb - 1 - i], state[2:], False)
        return (i + 1, live(new[:N_HEADS])) + new

    out = lax.while_loop(lambda state: jnp.logical_and(state[0] < qb, state[1]), step,
                         (jnp.int32(1), live(carry[:N_HEADS])) + carry)
    accs = out[2 + N_HEADS:]
    pairs = [jnp.where(first_rows, accs[PAIR * p], accs[PAIR * p + 1]) for p in range(N_HEADS // PAIR)]
    o_ref[...] = jnp.concatenate(
        [jnp.concatenate([t[:, c * LANES:(c + 1) * LANES].T for c in range(TQ // LANES)], axis=0) for t in pairs],
        axis=1).astype(o_ref.dtype)


def _sb(pa, batch, seq):
    TQ = SB_TILE
    nq = seq // TQ
    W = BRANCH_WIDTH
    return pl.pallas_call(
        functools.partial(_sb_kernel, seq=seq),
        grid=(batch, nq),
        in_specs=[pl.BlockSpec((TQ, W), lambda b, i: (b * nq + i, 3)),
                  pl.BlockSpec((seq, W), lambda b, i: (b, 4)),
                  pl.BlockSpec((seq, W), lambda b, i: (b, 5))],
        out_specs=pl.BlockSpec((TQ, W), lambda b, i: (b * nq + i, 0)),
        out_shape=jax.ShapeDtypeStruct((batch * seq, W), BF16),
        scratch_shapes=[pltpu.VMEM((W, seq), BF16)],
        compiler_params=_params(("parallel", "arbitrary")),
        name="sb",
    )(pa, pa, pa)


def _merge_kernel(x_ref, ya_ref, yb_ref, yc_ref, yd_ref, gpre_ref, wgate_ref, wbr_ref, wout_ref, gpost_ref, o_ref):
    rows = x_ref.shape[0] // TOKEN_SPLIT
    for part in range(TOKEN_SPLIT):
        r = slice(part * rows, (part + 1) * rows)
        x = x_ref[r, :]
        h = _rms(x, gpre_ref[...]).astype(BF16)
        merged = None
        for g, y_ref in enumerate((ya_ref, yb_ref, yc_ref, yd_ref)):
            gate = _sigmoid(_dot(h, wgate_ref[g]))
            term = gate * _dot(y_ref[r, :], wbr_ref[g])
            merged = term if merged is None else merged + term
        mix = _dot(merged.astype(BF16), wout_ref[...])
        o_ref[r, :] = x + _rms(mix, gpost_ref[...])


def _merge(x, ya, yb, yc, yd, gpre, wgate, wbr, wout, gpost):
    n = x.shape[0]
    tm = PART_ROWS * TOKEN_SPLIT
    row = lambda w: pl.BlockSpec((tm, w), lambda i: (i, 0))
    return pl.pallas_call(
        _merge_kernel,
        grid=(n // tm,),
        in_specs=[row(D_MODEL)] + [row(BRANCH_WIDTH)] * N_BRANCH
        + [_const_spec(gpre.shape), _const_spec(wgate.shape), _const_spec(wbr.shape),
           _const_spec(wout.shape), _const_spec(gpost.shape)],
        out_specs=row(D_MODEL),
        out_shape=jax.ShapeDtypeStruct((n, D_MODEL), F32),
        compiler_params=_params(("parallel",)),
        name="merge",
    )(x, ya, yb, yc, yd, gpre, wgate, wbr, wout, gpost)


def _ffn_kernel(x_ref, gpre_ref, wup_ref, wdown_ref, gpost_ref, o_ref):
    rows = x_ref.shape[0] // TOKEN_SPLIT
    for part in range(TOKEN_SPLIT):
        r = slice(part * rows, (part + 1) * rows)
        x = x_ref[r, :]
        h = _rms(x, gpre_ref[...]).astype(BF16)
        f = None
        for c in range(D_FF // FF_CHUNK):
            u = _dot(h, wup_ref[:, c * FF_CHUNK:(c + 1) * FF_CHUNK])
            a = jnp.square(jnp.maximum(u, 0.0)).astype(BF16)
            t = _dot(a, wdown_ref[c * FF_CHUNK:(c + 1) * FF_CHUNK, :])
            f = t if f is None else f + t
        o_ref[r, :] = x + _rms(f, gpost_ref[...])


def _ffn(x, gpre, wup, wdown, gpost):
    n = x.shape[0]
    tm = PART_ROWS * TOKEN_SPLIT
    row = pl.BlockSpec((tm, D_MODEL), lambda i: (i, 0))
    return pl.pallas_call(
        _ffn_kernel,
        grid=(n // tm,),
        in_specs=[row, _const_spec(gpre.shape), _const_spec(wup.shape), _const_spec(wdown.shape),
                  _const_spec(gpost.shape)],
        out_specs=row,
        out_shape=jax.ShapeDtypeStruct((n, D_MODEL), F32),
        compiler_params=_params(("parallel",)),
        name="ffn",
    )(x, gpre, wup, wdown, gpost)


def _pad_lanes(a, width):
    return jnp.pad(a, ((0, 0), (0, width - a.shape[1])))


def _small_groups(w_in, layer):
    main = 4 * BRANCH_WIDTH
    gdn_in = main + 2 * N_HEADS
    wg_small = _pad_lanes(w_in[layer, :, main:gdn_in], SMALL_PAD)
    ws_small = _pad_lanes(w_in[layer, :, gdn_in + ATT_W + main:], SMALL_PAD)
    return wg_small.astype(BF16), ws_small.astype(BF16)


def kernel(x, norm_mix_pre, norm_mix_post, norm_ffn_pre, norm_ffn_post, w_in, gdn_conv, gdn_a_log, gdn_dt_bias,
           gdn_norm, ssd_conv, ssd_conv_bias, ssd_a_log, ssd_dt_bias, ssd_d, ssd_norm, w_gate, w_branch, w_out,
           w_up, w_down):
    batch, seq, d = x.shape
    assert d == D_MODEL and seq % MOBA_BLOCK == 0
    assert (batch * seq) % TOKEN_TILE == 0 and (batch * seq) % (PART_ROWS * TOKEN_SPLIT) == 0
    depth = w_in.shape[0]
    xf = x.reshape(batch * seq, d).astype(F32)
    row = lambda a: a.reshape(1, -1).astype(F32)
    for l in range(depth):
        wg_small, ws_small = _small_groups(w_in, l)
        pg, pa, ps = _in_proj(xf, row(norm_mix_pre[l]), w_in.astype(F32), l, wg_small, ws_small)
        gdn_a = _pad_lanes(jnp.concatenate([jnp.zeros((N_HEADS,), F32), gdn_a_log[l]]).reshape(1, -1), SMALL_PAD)
        gdn_b = _pad_lanes(jnp.concatenate([jnp.zeros((N_HEADS,), F32), gdn_dt_bias[l]]).reshape(1, -1), SMALL_PAD)
        ya = _gdn(pg, gdn_conv[l].astype(F32), gdn_a, gdn_b, row(jnp.tile(gdn_norm[l], N_HEADS)), batch, seq)
        yb = _moba(pa, batch, seq)
        yc = _sb(pa, batch, seq)
        yd = _ssd(ps, ssd_conv[l].astype(F32), row(ssd_conv_bias[l]),
                  _pad_lanes(row(ssd_a_log[l]), SMALL_PAD), _pad_lanes(row(ssd_dt_bias[l]), SMALL_PAD),
                  row(jnp.repeat(ssd_d[l], HEAD_DIM)), row(ssd_norm[l]), batch, seq)
        xf = _merge(xf, ya, yb, yc, yd, row(norm_mix_pre[l]), w_gate[l].astype(BF16), w_branch[l].astype(BF16),
                    w_out[l].astype(BF16), row(norm_mix_post[l]))
        xf = _ffn(xf, row(norm_ffn_pre[l]), w_up[l].astype(BF16), w_down[l].astype(BF16), row(norm_ffn_post[l]))
    return xf.reshape(batch, seq, d).astype(x.dtype)
```

```python
import functools

import jax
import jax.numpy as jnp
from jax import lax
from jax.experimental import pallas as pl
from jax.experimental.pallas import tpu as pltpu

F32 = jnp.float32
BF16 = jnp.bfloat16
HIGHEST = lax.Precision.HIGHEST

D_MODEL = 1024
N_BRANCH = 4
BRANCH_WIDTH = 256
HEAD_DIM = 64
N_HEADS = 4
CONV_WIDTH = 4
GDN_CHUNK = 64
MOBA_BLOCK = 256
MOBA_TOPK = 3
SSD_STATE = 128
SSD_GROUPS = 2
SSD_CHUNK = 128
D_FF = 4 * D_MODEL
EPS = 1e-6

LANES = 128
SUBLANES = 8
PAIR = LANES // HEAD_DIM
SMALL_PAD = LANES
CONV_HALO = 8
TOKEN_TILE = 512
PART_ROWS = 256
TOKEN_SPLIT = 4
FF_CHUNK = 1024
SB_TILE = 256
LOG2E = 1.4426950408889634
SB_DEAD_TAIL = 152.0
SSD_UNROLL = 4
GDN_GROUP = 2
GDN_UNROLL = 8
VMEM_LIMIT = 56 * 1024 * 1024

GDN_W = 3 * BRANCH_WIDTH + BRANCH_WIDTH + SMALL_PAD
ATT_W = 6 * BRANCH_WIDTH
SSD_CONV_DIM = BRANCH_WIDTH + 2 * SSD_GROUPS * SSD_STATE
SSD_W = SSD_CONV_DIM + BRANCH_WIDTH + SMALL_PAD


def _dot(a, b):
    return jnp.dot(a, b, preferred_element_type=F32)


def _bdot(a, b):
    return _dot(a.astype(BF16), b.astype(BF16))


def _bdot_nt(a, b):
    return lax.dot_general(a.astype(BF16), b.astype(BF16), (((1,), (1,)), ((), ())),
                           preferred_element_type=F32)


def _hdot(a, b):
    return jnp.dot(a, b, preferred_element_type=F32, precision=HIGHEST)


def _split(x, pieces):
    out = []
    for _ in range(pieces - 1):
        p = x.astype(BF16)
        out.append(p)
        x = x - p.astype(F32)
    out.append(x.astype(BF16))
    return out


def _dot_data_const(x, m, pieces):
    acc = None
    for p in _split(x, pieces):
        t = _dot(p, m)
        acc = t if acc is None else acc + t
    return acc


def _dot_const_data(m, x, pieces):
    acc = None
    for p in _split(x, pieces):
        t = _dot(m, p)
        acc = t if acc is None else acc + t
    return acc


def _sigmoid(x):
    return 1.0 / (1.0 + jnp.exp(-x))


def _silu(x):
    return x * _sigmoid(x)


def _softplus(x):
    return jnp.maximum(x, 0.0) + jnp.log(1.0 + jnp.exp(-jnp.abs(x)))


def _rms(x, gain):
    return x * lax.rsqrt(jnp.mean(x * x, axis=-1, keepdims=True) + EPS) * gain


def _params(sem):
    return pltpu.CompilerParams(dimension_semantics=sem, vmem_limit_bytes=VMEM_LIMIT)


def _const_spec(shape):
    nd = len(shape)
    return pl.BlockSpec(shape, lambda *_: (0,) * nd, pipeline_mode=pl.Buffered(1))


def _cols(w_ref, start, width):
    base = start // LANES * LANES
    stop = min(-(-(start + width) // LANES) * LANES, w_ref.shape[1])
    return w_ref[:, base:stop][:, start - base:start - base + width].astype(BF16)


def _in_proj_kernel(x_ref, gain_ref, w_ref, wgs_ref, wss_ref, og_ref, oa_ref, os_ref, wg_ref, wa_ref, ws_ref):
    W = BRANCH_WIDTH
    main = 4 * W

    @pl.when(pl.program_id(0) == 0)
    def _():
        off_att = main + 2 * N_HEADS
        off_ssd = off_att + ATT_W
        for c in range(main // W):
            wg_ref[:, c * W:(c + 1) * W] = _cols(w_ref, c * W, W)
        for c in range(ATT_W // W):
            wa_ref[:, c * W:(c + 1) * W] = _cols(w_ref, off_att + c * W, W)
        for c in range(SSD_CONV_DIM // W):
            ws_ref[:, c * W:(c + 1) * W] = _cols(w_ref, off_ssd + W + c * W, W)
        ws_ref[:, SSD_CONV_DIM:main] = _cols(w_ref, off_ssd, W)

    h = _rms(x_ref[...], gain_ref[...]).astype(BF16)
    og_ref[:, :main] = _dot(h, wg_ref[...])
    og_ref[:, main:] = _dot(h, wgs_ref[...])
    oa_ref[...] = _dot(h, wa_ref[...]).astype(BF16)
    os_ref[:, :main] = _dot(h, ws_ref[...])
    os_ref[:, main:] = _dot(h, wss_ref[...])


def _in_proj(x, gain, w_in, layer, wg_small, ws_small):
    n = x.shape[0]
    tm = TOKEN_TILE
    row = lambda w: pl.BlockSpec((tm, w), lambda i: (i, 0))
    main = 4 * BRANCH_WIDTH
    return pl.pallas_call(
        _in_proj_kernel,
        grid=(n // tm,),
        in_specs=[row(D_MODEL), _const_spec((1, D_MODEL)),
                  pl.BlockSpec((None,) + w_in.shape[1:], lambda i: (layer, 0, 0), pipeline_mode=pl.Buffered(1)),
                  _const_spec(wg_small.shape), _const_spec(ws_small.shape)],
        out_specs=[row(GDN_W), row(ATT_W), row(SSD_W)],
        out_shape=[jax.ShapeDtypeStruct((n, GDN_W), F32), jax.ShapeDtypeStruct((n, ATT_W), BF16),
                   jax.ShapeDtypeStruct((n, SSD_W), F32)],
        scratch_shapes=[pltpu.VMEM((D_MODEL, main), BF16), pltpu.VMEM((D_MODEL, ATT_W), BF16),
                        pltpu.VMEM((D_MODEL, main), BF16)],
        compiler_params=_params(("arbitrary",)),
        name="in_proj",
    )(x, gain, w_in, wg_small, ws_small)


def _conv_silu_into(src_ref, w_ref, bias, dst_ref, seq, chunk):
    w = w_ref[...]
    taps = [w[k:k + 1, :] for k in range(CONV_WIDTH)]

    def finish(acc):
        if bias is not None:
            acc = acc + bias
        return _silu(acc)

    blk = src_ref[0:chunk, :]
    rows = lax.broadcasted_iota(jnp.int32, (chunk, 1), 0)
    acc = taps[CONV_WIDTH - 1] * blk
    for k in range(CONV_WIDTH - 1):
        shift = CONV_WIDTH - 1 - k
        acc = acc + taps[k] * jnp.where(rows >= shift, pltpu.roll(blk, shift, 0), 0.0)
    dst_ref[0:chunk, :] = finish(acc)

    def body(c, carry):
        r0 = pl.multiple_of(c * chunk, chunk)
        blk = src_ref[pl.ds(r0 - CONV_HALO, chunk + CONV_HALO), :]
        acc = taps[CONV_WIDTH - 1] * blk[CONV_HALO:]
        for k in range(CONV_WIDTH - 1):
            acc = acc + taps[k] * pltpu.roll(blk, CONV_WIDTH - 1 - k, 0)[CONV_HALO:]
        dst_ref[pl.ds(r0, chunk), :] = finish(acc)
        return carry

    lax.fori_loop(1, seq // chunk, body, 0)


def _block_diag(x, mask):
    return jnp.where(mask, jnp.concatenate([x] * N_HEADS, axis=0), jnp.zeros((), x.dtype))


def _dot1_bd(a, b, mask):
    return _dot(a.astype(BF16), _block_diag(b.astype(BF16), mask))


def _dot3_bd(a, b, mask):
    ah, al = _split(a, 2)
    bh, bl = _split(b, 2)
    bdh = _block_diag(bh, mask)
    return _dot(ah, bdh) + _dot(al, bdh) + _dot(ah, _block_diag(bl, mask))


def _gdn_prepare_kernel(qkv_ref, bg_ref, convw_ref, alog_ref, dtb_ref,
                        u_ref, wq_ref, aqk_ref, kdt_ref, gl_ref, conv_ref, *, seq):
    C = GDN_CHUNK
    W = BRANCH_WIDTH
    NC = seq // C
    _conv_silu_into(qkv_ref, convw_ref, None, conv_ref, seq, 4 * C)

    def iota(shape, dim):
        return lax.broadcasted_iota(jnp.int32, shape, dim)

    same_head = (iota((W, W), 0) // HEAD_DIM) == (iota((W, W), 1) // HEAD_DIM)
    same_head2 = jnp.concatenate([same_head, same_head], axis=1)
    seg_ones = jnp.where(same_head, 1.0, 0.0).astype(BF16)
    r_c = iota((C, W), 0)
    j_c = iota((C, W), 1) % HEAD_DIM
    incl = r_c >= j_c
    strict = r_c > j_c
    eye = jnp.where(r_c == j_c, 1.0, 0.0)
    blocks = [(r_c // n) == (j_c // n) for n in (SUBLANES, 2 * SUBLANES, 4 * SUBLANES, C)]
    tril = jnp.where(iota((C, C), 0) >= iota((C, C), 1), 1.0, 0.0).astype(BF16)
    lane = iota((1, LANES), 1)
    ex_r = iota((LANES, 2 * W), 0)
    ex_c = iota((LANES, 2 * W), 1)
    expand = jnp.where(ex_r == jnp.where(ex_c < W, ex_c // HEAD_DIM, N_HEADS + (ex_c - W) // HEAD_DIM),
                       1.0, 0.0).astype(BF16)
    neg_a = -jnp.exp(alog_ref[...])
    dtb = dtb_ref[...]
    scale = HEAD_DIM ** -0.5

    def row_form(gc):
        t = jnp.concatenate([gc, jnp.zeros((LANES - C, LANES), F32)], axis=0).T
        heads_rows = [t[N_HEADS + h:N_HEADS + h + 1, :] for h in range(N_HEADS)]
        tiles = [jnp.where(lane < HEAD_DIM, heads_rows[PAIR * p_], pltpu.roll(heads_rows[PAIR * p_ + 1], HEAD_DIM, 1))
                 for p_ in range(N_HEADS // PAIR)]
        return jnp.broadcast_to(jnp.concatenate(tiles, axis=1), (C, W))

    def prepare(i, carry):
        cs = [i * GDN_UNROLL + n for n in range(GDN_UNROLL)]
        r0s = [pl.multiple_of(c * C, C) for c in cs]
        each = lambda f, *xs: [f(*a) for a in zip(*xs)]
        qkv = [conv_ref[pl.ds(r0, C), :] for r0 in r0s]
        q = [a[:, :W] for a in qkv]
        k = [a[:, W:2 * W] for a in qkv]
        v = [a[:, 2 * W:] for a in qkv]
        ss = each(lambda q_, k_: _dot_data_const(jnp.concatenate([q_ * q_, k_ * k_], axis=0), seg_ones, 2), q, k)
        qn = each(lambda q_, s_: q_ * lax.rsqrt(s_[:C] + EPS) * scale, q, ss)
        kn = each(lambda k_, s_: k_ * lax.rsqrt(s_[C:] + EPS), k, ss)
        bg = [bg_ref[pl.ds(r0, C), :] for r0 in r0s]
        gc = each(lambda b_: _dot_const_data(tril, neg_a * _softplus(b_ + dtb), 3), bg)
        ex = each(lambda b_, g_: _dot_data_const(jnp.where(lane < N_HEADS, _sigmoid(b_), g_), expand, 3), bg, gc)
        beta = [a[:, :W] for a in ex]
        gcc = [a[:, W:] for a in ex]
        gcr = each(row_form, gc)
        dm = each(lambda a_, b_: jnp.exp(jnp.where(incl, a_ - b_, -jnp.inf)), gcc, gcr)
        kb = each(lambda a_, b_: a_ * b_, kn, beta)
        vb = each(lambda a_, b_: a_ * b_, v, beta)
        qk = each(lambda q_, kb_, kn_: lax.dot_general(
            jnp.concatenate([q_, kb_], axis=0).astype(BF16), _block_diag(kn_.astype(BF16), same_head),
            (((1,), (1,)), ((), ())), preferred_element_type=F32), qn, kb, kn)
        a_qk = each(lambda x_, d_: x_[:C] * d_, qk, dm)
        a = each(lambda x_, d_: jnp.where(strict, x_[C:] * d_, 0.0), qk, dm)
        d = [jnp.where(blocks[0], a_, 0.0) for a_ in a]
        p = [eye - d_ for d_ in d]
        b = each(lambda d_: _dot3_bd(d_, d_, same_head), d)
        pb = each(lambda p_, b_: _dot3_bd(jnp.concatenate([p_, b_], axis=0), b_, same_head), p, b)
        p = each(lambda p_, x_: p_ + x_[:C], p, pb)
        p = each(lambda p_, x_: p_ + _dot3_bd(p_, x_[C:], same_head), p, pb)
        for inner, outer in zip(blocks[:-1], blocks[1:]):
            pe = each(lambda p_, a_: _dot1_bd(p_, jnp.where(outer & ~inner, a_, 0.0), same_head), p, a)
            p = each(lambda p_, x_: p_ - _dot1_bd(x_, p_, same_head), p, pe)
        t_inv = p
        e_gc = [jnp.exp(g_) for g_ in gcc]
        uw = each(lambda t_, vb_, kb_, e_: _dot3_bd(t_, jnp.concatenate([vb_, kb_ * e_], axis=1), same_head2),
                  t_inv, vb, kb, e_gc)
        for n, c in enumerate(cs):
            r0 = r0s[n]
            g_last = gcc[n][C - 1:C, :]
            u_ref[pl.ds(r0, C), :] = uw[n][:, :W]
            wq_ref[pl.ds(pl.multiple_of(c * 2 * C, 2 * C), 2 * C), :] = jnp.concatenate(
                [uw[n][:, W:], qn[n] * e_gc[n]], axis=0).astype(BF16)
            aqk_ref[pl.ds(r0, C), :] = a_qk[n].astype(BF16)
            k_dec = jnp.concatenate([kn[n] * jnp.exp(g_last - gcc[n]), jnp.zeros((LANES - C, W), F32)], axis=0)
            kdt_ref[pl.ds(pl.multiple_of(c * W, W), W), :] = jnp.concatenate(
                [k_dec[:, :LANES].T, k_dec[:, LANES:].T], axis=0).astype(BF16)
            gl_ref[c] = jnp.broadcast_to(jnp.exp(g_last), (SUBLANES, W))
        return carry

    lax.fori_loop(0, NC // GDN_UNROLL, prepare, 0)


def _gdn_recur_kernel(u_ref, wq_ref, aqk_ref, kdt_ref, gl_ref, z_ref, norm_ref, o_ref, s_ref, *, seq, group):
    C = GDN_CHUNK
    W = BRANCH_WIDTH
    NC = seq // C
    same_head = ((lax.broadcasted_iota(jnp.int32, (W, W), 0) // HEAD_DIM)
                 == (lax.broadcasted_iota(jnp.int32, (W, W), 1) // HEAD_DIM))
    seg_ones = jnp.where(same_head, 1.0, 0.0).astype(BF16)
    gain = norm_ref[...]
    seqs = range(group)
    s_ref[...] = jnp.zeros_like(s_ref)

    def recur(c, carry):
        rows = [pl.multiple_of(g * seq + c * C, C) for g in seqs]
        rows2 = [pl.multiple_of(2 * (g * seq + c * C), 2 * C) for g in seqs]
        rows_k = [pl.multiple_of((g * NC + c) * W, W) for g in seqs]
        s = [s_ref[g] for g in seqs]
        ws_qs = [_dot(wq_ref[pl.ds(rows2[g], 2 * C), :], s[g].astype(BF16)) for g in seqs]
        v_new = [(u_ref[pl.ds(rows[g], C), :] - ws_qs[g][:C]).astype(BF16) for g in seqs]
        upd = [_dot(kdt_ref[pl.ds(rows_k[g], W), :],
                    jnp.concatenate([v_new[g], jnp.zeros((LANES - C, W), BF16)], axis=0)) for g in seqs]
        for g in seqs:
            s_ref[g] = s[g] * gl_ref[g * NC + c][0:1, :] + jnp.where(same_head, upd[g], 0.0)
        o = [ws_qs[g][C:] + _dot(aqk_ref[pl.ds(rows[g], C), :], _block_diag(v_new[g], same_head)) for g in seqs]
        ms = [_dot_data_const(o_ * o_, seg_ones, 2) * (1.0 / HEAD_DIM) for o_ in o]
        for g in seqs:
            y = o[g] * lax.rsqrt(ms[g] + EPS) * gain * _silu(z_ref[pl.ds(rows[g], C), :])
            o_ref[pl.ds(rows[g], C), :] = y.astype(o_ref.dtype)
        return carry

    lax.fori_loop(0, NC, recur, 0)


def _gdn(pg, conv_w, a_row, dtb_row, norm_row, batch, seq):
    C = GDN_CHUNK
    W = BRANCH_WIDTH
    nc = seq // C
    n = batch * seq
    u, wq, aqk, kdt, gl = pl.pallas_call(
        functools.partial(_gdn_prepare_kernel, seq=seq),
        grid=(batch,),
        in_specs=[pl.BlockSpec((seq, 3 * W), lambda b: (b, 0)),
                  pl.BlockSpec((seq, SMALL_PAD), lambda b: (b, 4 * W // SMALL_PAD)),
                  _const_spec(conv_w.shape), _const_spec(a_row.shape), _const_spec(dtb_row.shape)],
        out_specs=[pl.BlockSpec((seq, W), lambda b: (b, 0)),
                   pl.BlockSpec((2 * seq, W), lambda b: (b, 0)),
                   pl.BlockSpec((seq, W), lambda b: (b, 0)),
                   pl.BlockSpec((nc * W, LANES), lambda b: (b, 0)),
                   pl.BlockSpec((nc, SUBLANES, W), lambda b: (b, 0, 0))],
        out_shape=[jax.ShapeDtypeStruct((n, W), F32),
                   jax.ShapeDtypeStruct((2 * n, W), BF16),
                   jax.ShapeDtypeStruct((n, W), BF16),
                   jax.ShapeDtypeStruct((batch * nc * W, LANES), BF16),
                   jax.ShapeDtypeStruct((batch * nc, SUBLANES, W), F32)],
        scratch_shapes=[pltpu.VMEM((seq, 3 * W), F32)],
        compiler_params=_params(("parallel",)),
        name="gdn_prepare",
    )(pg, pg, conv_w, a_row, dtb_row)
    group = GDN_GROUP if batch % GDN_GROUP == 0 else 1
    return pl.pallas_call(
        functools.partial(_gdn_recur_kernel, seq=seq, group=group),
        grid=(batch // group,),
        in_specs=[pl.BlockSpec((group * seq, W), lambda b: (b, 0)),
                  pl.BlockSpec((2 * group * seq, W), lambda b: (b, 0)),
                  pl.BlockSpec((group * seq, W), lambda b: (b, 0)),
                  pl.BlockSpec((group * nc * W, LANES), lambda b: (b, 0)),
                  pl.BlockSpec((group * nc, SUBLANES, W), lambda b: (b, 0, 0)),
                  pl.BlockSpec((group * seq, W), lambda b: (b, 3)),
                  _const_spec(norm_row.shape)],
        out_specs=pl.BlockSpec((group * seq, W), lambda b: (b, 0)),
        out_shape=jax.ShapeDtypeStruct((n, W), BF16),
        scratch_shapes=[pltpu.VMEM((group, W, W), F32)],
        compiler_params=_params(("parallel",)),
        name="gdn_recur",
    )(u, wq, aqk, kdt, gl, pg, norm_row)


def _ssd_kernel(xbc_ref, z_ref, dt_ref, convw_ref, convb_ref, alog_ref, dtb_ref, dskip_ref, norm_ref,
                o_ref, conv_ref, s_ref, *, seq):
    L = SSD_CHUNK
    W = BRANCH_WIDTH
    GN = SSD_GROUPS * SSD_STATE
    _conv_silu_into(xbc_ref, convw_ref, convb_ref[...], conv_ref, seq, L)
    s_ref[...] = jnp.zeros_like(s_ref)

    ri = lax.broadcasted_iota(jnp.int32, (L, L), 0)
    ci = lax.broadcasted_iota(jnp.int32, (L, L), 1)
    incl = ri >= ci
    tril = jnp.where(incl, 1.0, 0.0)
    first = lax.broadcasted_iota(jnp.int32, (1, LANES), 1) < HEAD_DIM
    neg_a = -jnp.exp(alog_ref[...])
    dtb = dtb_ref[...]
    dskip = dskip_ref[...]
    gain = norm_ref[...]

    groups = range(SSD_GROUPS)
    units = [(n, g) for n in range(SSD_UNROLL) for g in groups]
    pair_col = lambda t, g: jnp.where(first, t[:, PAIR * g:PAIR * g + 1], t[:, PAIR * g + 1:PAIR * g + 2])

    def body(i, carry):
        r0 = [pl.multiple_of((i * SSD_UNROLL + n) * L, L) for n in range(SSD_UNROLL)]
        xbc = [conv_ref[pl.ds(r, L), :] for r in r0]
        dt = [_softplus(dt_ref[pl.ds(r, L), :] + dtb) for r in r0]
        acs = [_hdot(tril, dt_ * neg_a) for dt_ in dt]
        acs_rows = [a_.T for a_ in acs]
        wide = lambda t, h: jnp.broadcast_to(t[:, h:h + 1], (L, LANES))
        acs_w = {(n, h): wide(acs[n], h) for n in range(SSD_UNROLL) for h in range(N_HEADS)}
        acs_p = {(n, g): jnp.where(first, acs_w[n, PAIR * g], acs_w[n, PAIR * g + 1]) for n, g in units}
        dt_p = {(n, g): jnp.where(first, wide(dt[n], PAIR * g), wide(dt[n], PAIR * g + 1)) for n, g in units}
        a_last = {u: acs_p[u][L - 1:L, :] for u in units}
        x2 = {(n, g): xbc[n][:, g * LANES:(g + 1) * LANES] for n, g in units}
        bm = {(n, g): xbc[n][:, W + g * SSD_STATE:W + (g + 1) * SSD_STATE] for n, g in units}
        cm = {(n, g): xbc[n][:, W + GN + g * SSD_STATE:W + GN + (g + 1) * SSD_STATE] for n, g in units}
        cb = {u: _bdot_nt(cm[u], bm[u]) for u in units}
        grow = {u: _bdot(bm[u].T, x2[u] * (dt_p[u] * jnp.exp(a_last[u] - acs_p[u]))) for u in units}
        s = {g: s_ref[g] for g in groups}
        y_off = {}
        for n, g in units:
            y_off[n, g] = _bdot(cm[n, g], s[g]) * jnp.exp(acs_p[n, g])
            s[g] = s[g] * jnp.exp(a_last[n, g]) + grow[n, g]
        for g in groups:
            s_ref[g] = s[g]
        xdt = {u: x2[u] * dt_p[u] for u in units}
        yd = {}
        for n, g in units:
            for h in (PAIR * g, PAIR * g + 1):
                lmat = jnp.exp(jnp.where(incl, acs_w[n, h] - acs_rows[n][h:h + 1, :], -jnp.inf))
                yd[n, h] = _bdot(cb[n, g] * lmat, xdt[n, g])
        for n in range(SSD_UNROLL):
            ys = [jnp.where(first, yd[n, PAIR * g], yd[n, PAIR * g + 1]) + y_off[n, g] for g in groups]
            y = jnp.concatenate(ys, axis=1) + dskip * xbc[n][:, :W]
            zt = z_ref[pl.ds(r0[n], L), :]
            o_ref[pl.ds(r0[n], L), :] = _rms(y * _silu(zt), gain).astype(o_ref.dtype)
        return carry

    lax.fori_loop(0, seq // (L * SSD_UNROLL), body, 0)


def _ssd(ps, conv_w, conv_b, a_row, dtb_row, dskip_row, norm_row, batch, seq):
    kern = functools.partial(_ssd_kernel, seq=seq)
    return pl.pallas_call(
        kern,
        grid=(batch,),
        in_specs=[pl.BlockSpec((seq, SSD_CONV_DIM), lambda b: (b, 0)),
                  pl.BlockSpec((seq, BRANCH_WIDTH), lambda b: (b, SSD_CONV_DIM // BRANCH_WIDTH)),
                  pl.BlockSpec((seq, SMALL_PAD), lambda b: (b, (SSD_CONV_DIM + BRANCH_WIDTH) // SMALL_PAD)),
                  _const_spec(conv_w.shape), _const_spec(conv_b.shape), _const_spec(a_row.shape),
                  _const_spec(dtb_row.shape), _const_spec(dskip_row.shape), _const_spec(norm_row.shape)],
        out_specs=pl.BlockSpec((seq, BRANCH_WIDTH), lambda b: (b, 0)),
        out_shape=jax.ShapeDtypeStruct((batch * seq, BRANCH_WIDTH), BF16),
        scratch_shapes=[pltpu.VMEM((seq, SSD_CONV_DIM), F32),
                        pltpu.VMEM((SSD_GROUPS, SSD_STATE, LANES), F32)],
        compiler_params=_params(("parallel",)),
        name="ssd",
    )(ps, ps, ps, conv_w, conv_b, a_row, dtb_row, dskip_row, norm_row)


def _head_masked(q, first_half):
    lane = lax.broadcasted_iota(jnp.int32, (1, LANES), 1)
    keep = (lane < HEAD_DIM) if first_half else (lane >= HEAD_DIM)
    return jnp.where(keep, q.astype(F32), 0.0)


def _moba_kernel(q_ref, k_ref, v_ref, o_ref, kmean_ref, vt_ref, *, seq):
    BS = MOBA_BLOCK
    NB = seq // BS
    W = BRANCH_WIDTH
    qb = pl.program_id(1)
    scale = HEAD_DIM ** -0.5
    q = q_ref[...]

    @pl.when(qb == 0)
    def _():
        kmean = jnp.sum(k_ref[...].astype(F32).reshape(NB, BS, W), axis=1) * (1.0 / BS)
        kmean_ref[...] = jnp.concatenate([kmean, jnp.zeros((SUBLANES, W), F32)], axis=0)
        for c in range(seq // LANES):
            blk = v_ref[c * LANES:(c + 1) * LANES, :].astype(F32)
            vt_ref[:, c * LANES:(c + 1) * LANES] = jnp.concatenate(
                [blk[:, :LANES].T, blk[:, LANES:].T], axis=0).astype(BF16)

    rel = (lax.broadcasted_iota(jnp.int32, (BS, BS), 1)
           - lax.broadcasted_iota(jnp.int32, (BS, BS), 0)).astype(F32)
    first_rows = lax.broadcasted_iota(jnp.int32, (LANES, 1), 0) < HEAD_DIM
    blk_id = lax.broadcasted_iota(jnp.int32, (SUBLANES, BS), 0)
    heads = range(N_HEADS)
    pair_of = [h // PAIR for h in heads]
    lanes_of = lambda x, p: x[:, p * LANES:(p + 1) * LANES]
    slopes = [2.0 ** (-8.0 * (h + 1) / N_HEADS) for h in heads]

    qf = [_head_masked(lanes_of(q, pair_of[h]), h % PAIR == 0) for h in heads]
    km = _split(kmean_ref[...], 3)
    gate = []
    for h in heads:
        qh = qf[h].astype(BF16)
        g_h = None
        for piece in km:
            t = lax.dot_general(lanes_of(piece, pair_of[h]), qh, (((1,), (1,)), ((), ())),
                                preferred_element_type=F32)
            g_h = t if g_h is None else g_h + t
        gate.append(g_h[:NB])
    sels = []
    for h in heads:
        rank = jnp.zeros((NB, BS), F32)
        for i in range(NB):
            gi = gate[h][i:i + 1, :]
            beats = jnp.where(gi > gate[h], 1.0, jnp.where((gi == gate[h]) & (blk_id > i), 1.0, 0.0))
            rank = rank + jnp.where(i < qb, beats, 0.0)
        sels.append(jnp.where((rank < MOBA_TOPK) & (blk_id < qb), 1.0, 0.0))
    qms = [(qf[h] * scale).astype(BF16) for h in heads]

    def blocks(js, carry, own_first):
        r0 = [pl.multiple_of(j * BS, BS) for j in js]
        k_j = [k_ref[pl.ds(r, BS), :] for r in r0]
        vt_j = [vt_ref[:, pl.ds(r, BS)] for r in r0]
        is_own = lambda b: own_first and b == 0
        dist = [rel if is_own(b) else rel + ((qb - j) * BS).astype(F32) for b, j in enumerate(js)]
        m, l, acc = carry[:N_HEADS], carry[N_HEADS:2 * N_HEADS], carry[2 * N_HEADS:]
        units = [(b, h) for b in range(len(js)) for h in heads]
        raw = {u: _bdot_nt(lanes_of(k_j[u[0]], pair_of[u[1]]), qms[u[1]]) for u in units}
        keep = {(b, h): (rel >= 0) if is_own(b) else
                jnp.sum(jnp.where(blk_id == js[b], sels[h], 0.0), axis=0, keepdims=True) > 0.5 for b, h in units}
        s = {(b, h): jnp.where(keep[b, h], raw[b, h] - slopes[h] * dist[b], -jnp.inf) for b, h in units}
        m_new = list(m)
        for b, h in units:
            m_blk = jnp.max(s[b, h], axis=0, keepdims=True)
            m_new[h] = m_blk if m_new[h] is None else jnp.maximum(m_new[h], m_blk)
        p = {(b, h): jnp.exp(s[b, h] - m_new[h]) for b, h in units}
        pv, l_new = [None] * N_HEADS, [None] * N_HEADS
        for b, h in units:
            t = _bdot(vt_j[b][pair_of[h] * LANES:(pair_of[h] + 1) * LANES, :], p[b, h])
            pv[h] = t if pv[h] is None else pv[h] + t
            t = jnp.sum(p[b, h], axis=0, keepdims=True)
            l_new[h] = t if l_new[h] is None else l_new[h] + t
        if not own_first:
            alpha = [jnp.exp(m[h] - m_new[h]) for h in heads]
            l_new = [l[h] * alpha[h] + l_new[h] for h in heads]
            pv = [acc[h] * alpha[h] + pv[h] for h in heads]
        return tuple(m_new) + tuple(l_new) + tuple(pv)

    none = (None,) * (3 * N_HEADS)
    state = lax.cond(qb >= 2, lambda: blocks([qb, 0, 1], none, True), lambda: blocks([qb], none, True))
    out = lax.fori_loop(1, qb // 2, lambda i, c: blocks([2 * i, 2 * i + 1], c, False), state)
    out = lax.cond(qb % 2 == 1, lambda c: blocks([qb - 1], c, False), lambda c: c, out)
    l, acc = out[N_HEADS:2 * N_HEADS], out[2 * N_HEADS:]
    o = [acc[h] / l[h] for h in heads]
    pairs = [jnp.where(first_rows, o[PAIR * p_], o[PAIR * p_ + 1]) for p_ in range(N_HEADS // PAIR)]
    o_ref[...] = jnp.concatenate(
        [jnp.concatenate([t[:, :LANES].T, t[:, LANES:].T], axis=0) for t in pairs], axis=1).astype(o_ref.dtype)


def _moba(pa, batch, seq):
    BS = MOBA_BLOCK
    nq = seq // BS
    assert nq == SUBLANES
    W = BRANCH_WIDTH
    kern = functools.partial(_moba_kernel, seq=seq)
    return pl.pallas_call(
        kern,
        grid=(batch, nq),
        in_specs=[pl.BlockSpec((BS, W), lambda b, i: (b * nq + i, 0)),
                  pl.BlockSpec((seq, W), lambda b, i: (b, 1)),
                  pl.BlockSpec((seq, W), lambda b, i: (b, 2))],
        out_specs=pl.BlockSpec((BS, W), lambda b, i: (b * nq + i, 0)),
        out_shape=jax.ShapeDtypeStruct((batch * seq, W), BF16),
        scratch_shapes=[pltpu.VMEM((2 * SUBLANES, W), F32),
                        pltpu.VMEM((W, seq), BF16)],
        compiler_params=_params(("parallel", "arbitrary")),
        name="moba",
    )(pa, pa, pa)


def _sb_kernel(q_ref, k_ref, v_ref, o_ref, vt_ref, *, seq):
    TQ = SB_TILE
    qb = pl.program_id(1)
    scale = HEAD_DIM ** -0.5
    q = q_ref[...]

    @pl.when(qb == 0)
    def _():
        for c in range(seq // LANES):
            blk = v_ref[c * LANES:(c + 1) * LANES, :].astype(F32)
            vt_ref[:, c * LANES:(c + 1) * LANES] = jnp.concatenate(
                [blk[:, :LANES].T, blk[:, LANES:].T], axis=0).astype(BF16)

    before = lax.broadcasted_iota(jnp.int32, (TQ, TQ), 0) < lax.broadcasted_iota(jnp.int32, (TQ, TQ), 1)
    later = jnp.where(before, 1.0, 0.0).astype(BF16)
    first_rows = lax.broadcasted_iota(jnp.int32, (LANES, 1), 0) < HEAD_DIM
    heads = range(N_HEADS)
    pair_of = [h // PAIR for h in heads]
    lanes_of = lambda x, p: x[:, p * LANES:(p + 1) * LANES]
    qms = [(_head_masked(lanes_of(q, pair_of[h]), h % PAIR == 0) * scale).astype(BF16) for h in heads]

    def tiles(kbs, carry, diagonal_first):
        r0 = [pl.multiple_of(kb * TQ, TQ) for kb in kbs]
        k_j = [k_ref[pl.ds(r, TQ), :] for r in r0]
        vt_j = [vt_ref[:, pl.ds(r, TQ)] for r in r0]
        tails, accs = list(carry[:N_HEADS]), list(carry[N_HEADS:])
        units = [(b, h) for b in range(len(kbs)) for h in heads]
        masked = lambda b: diagonal_first and b == 0
        z2 = {(b, h): _bdot_nt(lanes_of(k_j[b], pair_of[h]), qms[h]) * LOG2E for b, h in units}
        cost = {u: jnp.maximum(z2[u], 0.0) + jnp.log2(1.0 + jnp.exp2(-jnp.abs(z2[u]))) for u in units}
        cost = {u: jnp.where(before, cost[u], 0.0) if masked(u[0]) else cost[u] for u in units}
        after = {u: _dot_const_data(later, cost[u], 2) for u in units}
        w = {}
        for b, h in units:
            w_ = jnp.exp2(z2[b, h] - cost[b, h] - after[b, h] - tails[h])
            w[b, h] = jnp.where(before, w_, 0.0) if masked(b) else w_
            tails[h] = tails[h] + jnp.sum(cost[b, h], axis=0, keepdims=True)
        for b, h in units:
            accs[h] = accs[h] + _bdot(vt_j[b][pair_of[h] * LANES:(pair_of[h] + 1) * LANES, :], w[b, h])
        return tuple(tails) + tuple(accs)

    def live(tails):
        least = tails[0]
        for t_ in tails[1:]:
            least = jnp.minimum(least, t_)
        return jnp.min(least) < SB_DEAD_TAIL

    init = tuple(jnp.zeros((1, TQ), F32) for _ in heads) + tuple(jnp.zeros((LANES, TQ), F32) for _ in heads)
    carry = lax.cond(qb > 0, lambda c: tiles([qb, qb - 1], c, True), lambda c: tiles([qb], c, True), init)

    def step(state):
        i = state[0]
        new = tiles([qb - 1 - i], state[2:], False)
        return (i + 1, live(new[:N_HEADS])) + new

    out = lax.while_loop(lambda state: jnp.logical_and(state[0] < qb, state[1]), step,
                         (jnp.int32(1), live(carry[:N_HEADS])) + carry)
    accs = out[2 + N_HEADS:]
    pairs = [jnp.where(first_rows, accs[PAIR * p], accs[PAIR * p + 1]) for p in range(N_HEADS // PAIR)]
    o_ref[...] = jnp.concatenate(
        [jnp.concatenate([t[:, c * LANES:(c + 1) * LANES].T for c in range(TQ // LANES)], axis=0) for t in pairs],
        axis=1).astype(o_ref.dtype)


def _sb(pa, batch, seq):
    TQ = SB_TILE
    nq = seq // TQ
    W = BRANCH_WIDTH
    return pl.pallas_call(
        functools.partial(_sb_kernel, seq=seq),
        grid=(batch, nq),
        in_specs=[pl.BlockSpec((TQ, W), lambda b, i: (b * nq + i, 3)),
                  pl.BlockSpec((seq, W), lambda b, i: (b, 4)),
                  pl.BlockSpec((seq, W), lambda b, i: (b, 5))],
        out_specs=pl.BlockSpec((TQ, W), lambda b, i: (b * nq + i, 0)),
        out_shape=jax.ShapeDtypeStruct((batch * seq, W), BF16),
        scratch_shapes=[pltpu.VMEM((W, seq), BF16)],
        compiler_params=_params(("parallel", "arbitrary")),
        name="sb",
    )(pa, pa, pa)


def _merge_kernel(x_ref, ya_ref, yb_ref, yc_ref, yd_ref, gpre_ref, wgate_ref, wbr_ref, wout_ref, gpost_ref, o_ref):
    rows = x_ref.shape[0] // TOKEN_SPLIT
    for part in range(TOKEN_SPLIT):
        r = slice(part * rows, (part + 1) * rows)
        x = x_ref[r, :]
        h = _rms(x, gpre_ref[...]).astype(BF16)
        merged = None
        for g, y_ref in enumerate((ya_ref, yb_ref, yc_ref, yd_ref)):
            gate = _sigmoid(_dot(h, wgate_ref[g]))
            term = gate * _dot(y_ref[r, :], wbr_ref[g])
            merged = term if merged is None else merged + term
        mix = _dot(merged.astype(BF16), wout_ref[...])
        o_ref[r, :] = x + _rms(mix, gpost_ref[...])


def _merge(x, ya, yb, yc, yd, gpre, wgate, wbr, wout, gpost):
    n = x.shape[0]
    tm = PART_ROWS * TOKEN_SPLIT
    row = lambda w: pl.BlockSpec((tm, w), lambda i: (i, 0))
    return pl.pallas_call(
        _merge_kernel,
        grid=(n // tm,),
        in_specs=[row(D_MODEL)] + [row(BRANCH_WIDTH)] * N_BRANCH
        + [_const_spec(gpre.shape), _const_spec(wgate.shape), _const_spec(wbr.shape),
           _const_spec(wout.shape), _const_spec(gpost.shape)],
        out_specs=row(D_MODEL),
        out_shape=jax.ShapeDtypeStruct((n, D_MODEL), F32),
        compiler_params=_params(("parallel",)),
        name="merge",
    )(x, ya, yb, yc, yd, gpre, wgate, wbr, wout, gpost)


def _ffn_kernel(x_ref, gpre_ref, wup_ref, wdown_ref, gpost_ref, o_ref):
    rows = x_ref.shape[0] // TOKEN_SPLIT
    for part in range(TOKEN_SPLIT):
        r = slice(part * rows, (part + 1) * rows)
        x = x_ref[r, :]
        h = _rms(x, gpre_ref[...]).astype(BF16)
        f = None
        for c in range(D_FF // FF_CHUNK):
            u = _dot(h, wup_ref[:, c * FF_CHUNK:(c + 1) * FF_CHUNK])
            a = jnp.square(jnp.maximum(u, 0.0)).astype(BF16)
            t = _dot(a, wdown_ref[c * FF_CHUNK:(c + 1) * FF_CHUNK, :])
            f = t if f is None else f + t
        o_ref[r, :] = x + _rms(f, gpost_ref[...])


def _ffn(x, gpre, wup, wdown, gpost):
    n = x.shape[0]
    tm = PART_ROWS * TOKEN_SPLIT
    row = pl.BlockSpec((tm, D_MODEL), lambda i: (i, 0))
    return pl.pallas_call(
        _ffn_kernel,
        grid=(n // tm,),
        in_specs=[row, _const_spec(gpre.shape), _const_spec(wup.shape), _const_spec(wdown.shape),
                  _const_spec(gpost.shape)],
        out_specs=row,
        out_shape=jax.ShapeDtypeStruct((n, D_MODEL), F32),
        compiler_params=_params(("parallel",)),
        name="ffn",
    )(x, gpre, wup, wdown, gpost)


def _pad_lanes(a, width):
    return jnp.pad(a, ((0, 0), (0, width - a.shape[1])))


def _small_groups(w_in, layer):
    main = 4 * BRANCH_WIDTH
    gdn_in = main + 2 * N_HEADS
    wg_small = _pad_lanes(w_in[layer, :, main:gdn_in], SMALL_PAD)
    ws_small = _pad_lanes(w_in[layer, :, gdn_in + ATT_W + main:], SMALL_PAD)
    return wg_small.astype(BF16), ws_small.astype(BF16)


def kernel(x, norm_mix_pre, norm_mix_post, norm_ffn_pre, norm_ffn_post, w_in, gdn_conv, gdn_a_log, gdn_dt_bias,
           gdn_norm, ssd_conv, ssd_conv_bias, ssd_a_log, ssd_dt_bias, ssd_d, ssd_norm, w_gate, w_branch, w_out,
           w_up, w_down):
    batch, seq, d = x.shape
    assert d == D_MODEL and seq % MOBA_BLOCK == 0
    assert (batch * seq) % TOKEN_TILE == 0 and (batch * seq) % (PART_ROWS * TOKEN_SPLIT) == 0
    depth = w_in.shape[0]
    xf = x.reshape(batch * seq, d).astype(F32)
    row = lambda a: a.reshape(1, -1).astype(F32)
    for l in range(depth):
        wg_small, ws_small = _small_groups(w_in, l)
        pg, pa, ps = _in_proj(xf, row(norm_mix_pre[l]), w_in.astype(F32), l, wg_small, ws_small)
        gdn_a = _pad_lanes(jnp.concatenate([jnp.zeros((N_HEADS,), F32), gdn_a_log[l]]).reshape(1, -1), SMALL_PAD)
        gdn_b = _pad_lanes(jnp.concatenate([jnp.zeros((N_HEADS,), F32), gdn_dt_bias[l]]).reshape(1, -1), SMALL_PAD)
        ya = _gdn(pg, gdn_conv[l].astype(F32), gdn_a, gdn_b, row(jnp.tile(gdn_norm[l], N_HEADS)), batch, seq)
        yb = _moba(pa, batch, seq)
        yc = _sb(pa, batch, seq)
        yd = _ssd(ps, ssd_conv[l].astype(F32), row(ssd_conv_bias[l]),
                  _pad_lanes(row(ssd_a_log[l]), SMALL_PAD), _pad_lanes(row(ssd_dt_bias[l]), SMALL_PAD),
                  row(jnp.repeat(ssd_d[l], HEAD_DIM)), row(ssd_norm[l]), batch, seq)
        xf = _merge(xf, ya, yb, yc, yd, row(norm_mix_pre[l]), w_gate[l].astype(BF16), w_branch[l].astype(BF16),
                    w_out[l].astype(BF16), row(norm_mix_post[l]))
        xf = _ffn(xf, row(norm_ffn_pre[l]), w_up[l].astype(BF16), w_down[l].astype(BF16), row(norm_ffn_post[l]))
    return xf.reshape(batch, seq, d).astype(x.dtype)
```
